```python
import math
import jax
import jax.numpy as jnp
from jax import lax
import numpy as np

D_MODEL = 2048
BATCH = 1
SEQ = 16384
DEPTH = 4

F32 = jnp.float32
MIX_WIDTH = D_MODEL
N_MIXERS = 4
GROUP_WIDTH = MIX_WIDTH // N_MIXERS
SWA_HEADS = 8
SWA_KV_HEADS = 2
SWA_HEAD_DIM = GROUP_WIDTH // SWA_HEADS
WINDOW = 128
SWA_BLOCK = 128
CONV_CH = GROUP_WIDTH
CONV_WIDTH = 31
MLA_HEADS = 4
MLA_Q_LORA = 512
MLA_KV_LORA = 256
MLA_NOPE = 128
MLA_ROPE = 64
MLA_V = GROUP_WIDTH // MLA_HEADS
MLA_QBLK = 128
ROPE_THETA = 10000.0
SSM_CH = GROUP_WIDTH
SSM_GROUP_IN = 16
SSM_GROUPS = SSM_CH // SSM_GROUP_IN
SSM_STATE = 64
DT_MIN = 0.001
DT_MAX = 0.1
REL_BUCKETS = 32
REL_MAX_DIST = 128
MEM_TOKENS = 256
XA_HEADS = 4
XA_HEAD_DIM = 128
N_EXPERT_GROUPS = 4
EXPERTS_PER_GROUP = 8
N_EXPERTS = N_EXPERT_GROUPS * EXPERTS_PER_GROUP
EXPERT_TOPK = 2
EXPERT_FF = 512
EXPERT_BLOCK = 256
NORM_EPS = 1e-6
NEG_INF = -1e30
IN_SPLITS = (SWA_HEADS * SWA_HEAD_DIM, SWA_KV_HEADS * SWA_HEAD_DIM, SWA_KV_HEADS * SWA_HEAD_DIM,
             CONV_CH, CONV_CH, MLA_Q_LORA, MLA_KV_LORA, MLA_ROPE, SSM_CH)
IN_WIDTH = sum(IN_SPLITS)

kernel_name = 'hybrid_parallel_heads_hmoe_trunk'


def rms_norm(x, g):
    x32 = x.astype(F32)
    y = x32 * lax.rsqrt(jnp.mean(x32 * x32, axis=-1, keepdims=True) + NORM_EPS)
    return (y * g.astype(F32)).astype(x.dtype)


def layer_norm(x, g, b):
    x32 = x.astype(F32)
    xc = x32 - jnp.mean(x32, axis=-1, keepdims=True)
    y = xc * lax.rsqrt(jnp.mean(xc * xc, axis=-1, keepdims=True) + NORM_EPS)
    return (y * g.astype(F32) + b.astype(F32)).astype(x.dtype)


def rope_tables(seq_len):
    inv = 1.0 / (ROPE_THETA ** (jnp.arange(0, MLA_ROPE, 2, dtype=F32) / MLA_ROPE))
    ang = jnp.arange(seq_len, dtype=F32)[:, None] * inv[None, :]
    return jnp.cos(ang), jnp.sin(ang)


def apply_rope(x, cos, sin):
    x32 = x.astype(F32)
    x1, x2 = jnp.split(x32, 2, axis=-1)
    return jnp.concatenate([x1 * cos - x2 * sin, x1 * sin + x2 * cos], axis=-1).astype(x.dtype)


def t5_band_buckets():
    qi = np.arange(SWA_BLOCK)[:, None]
    kj = np.arange(2 * SWA_BLOCK)[None, :]
    dist = SWA_BLOCK + qi - kj
    valid = (dist >= 0) & (dist < WINDOW)
    max_exact = REL_BUCKETS // 2
    d0 = np.maximum(dist, 0)
    dl = np.maximum(dist, 1).astype(np.float32)
    large = max_exact + (np.log(dl / max_exact) / np.log(REL_MAX_DIST / max_exact)
                         * (REL_BUCKETS - max_exact)).astype(np.int32)
    large = np.minimum(large, REL_BUCKETS - 1)
    bucket = np.where(d0 < max_exact, d0, large).astype(np.int32)
    return bucket, valid


def swa_sink_attention(q, k, v, sinks, rel_bias):
    b, s, _, dh = q.shape
    nb = s // SWA_BLOCK
    grp = SWA_HEADS // SWA_KV_HEADS
    qb = q.reshape(b, nb, SWA_BLOCK, SWA_KV_HEADS, grp, dh)

    def band(t):
        tb = t.reshape(b, nb, SWA_BLOCK, SWA_KV_HEADS, dh)
        prev = jnp.concatenate([jnp.zeros_like(tb[:, :1]), tb[:, :-1]], axis=1)
        return jnp.concatenate([prev, tb], axis=2)

    kb, vb = band(k), band(v)
    bucket, valid = t5_band_buckets()
    bias = rel_bias.astype(F32)[bucket]
    bias = bias.reshape(SWA_BLOCK, 2 * SWA_BLOCK, SWA_KV_HEADS, grp).transpose(2, 3, 0, 1)
    not_first = (np.arange(nb)[:, None, None] > 0) | (np.arange(2 * SWA_BLOCK)[None, None, :] >= SWA_BLOCK)
    mask = jnp.asarray(valid[None] & not_first)[None, :, None, None]
    logits = jnp.einsum('bnqkgd,bnjkd->bnkgqj', qb, kb).astype(F32) * (dh ** -0.5) + bias
    logits = jnp.where(mask, logits, NEG_INF)
    sink = sinks.astype(F32).reshape(SWA_KV_HEADS, grp)[None, None, :, :, None, None]
    m = jnp.maximum(jnp.max(logits, axis=-1, keepdims=True), sink)
    p = jnp.exp(logits - m)
    probs = p / (jnp.sum(p, axis=-1, keepdims=True) + jnp.exp(sink - m))
    out = jnp.einsum('bnkgqj,bnjkd->bnqkgd', probs.astype(v.dtype), vb)
    return out.reshape(b, s, SWA_HEADS * dh)


def conformer_conv(a, g, dw_w, dw_b, ln_g, ln_b, pw_w, pw_b):
    z = a * jax.nn.sigmoid(g)
    z = lax.conv_general_dilated(z, dw_w[:, None, :].astype(z.dtype), window_strides=(1,),
                                 padding=[(CONV_WIDTH - 1, 0)],
                                 dimension_numbers=('NWC', 'WIO', 'NWC'),
                                 feature_group_count=CONV_CH) + dw_b
    z = jax.nn.silu(layer_norm(z, ln_g, ln_b))
    return z @ pw_w + pw_b


def mla_attention(q_lat, kv_lat, k_rope_in, qa_g, kva_g, wq_b, wkv_b,
                  qn_nope_g, qn_rope_g, kn_nope_g, kn_rope_g, cos, sin):
    b, s, _ = q_lat.shape
    q = (rms_norm(q_lat, qa_g) @ wq_b).reshape(b, s, MLA_HEADS, MLA_NOPE + MLA_ROPE)
    kv = (rms_norm(kv_lat, kva_g) @ wkv_b).reshape(b, s, MLA_HEADS, MLA_NOPE + MLA_V)
    q_nope = rms_norm(q[..., :MLA_NOPE], qn_nope_g)
    q_rope = apply_rope(rms_norm(q[..., MLA_NOPE:], qn_rope_g), cos[:, None, :], sin[:, None, :])
    k_nope = rms_norm(kv[..., :MLA_NOPE], kn_nope_g)
    v = kv[..., MLA_NOPE:]
    k_rope = apply_rope(rms_norm(k_rope_in, kn_rope_g), cos, sin)
    nb = s // MLA_QBLK

    def to_blocks(t):
        return jnp.moveaxis(t.reshape(b, nb, MLA_QBLK, *t.shape[2:]), 1, 0)

    q_pos = jnp.arange(s, dtype=jnp.int32).reshape(nb, MLA_QBLK)
    k_pos = jnp.arange(s, dtype=jnp.int32)
    scale = (MLA_NOPE + MLA_ROPE) ** -0.5

    def one_block(args):
        qn, qr, qp = args
        logits = (jnp.einsum('bqhd,bkhd->bhqk', qn, k_nope)
                  + jnp.einsum('bqhr,bkr->bhqk', qr, k_rope)).astype(F32) * scale
        logits = jnp.where(k_pos[None, :] <= qp[:, None], logits, NEG_INF)
        probs = jax.nn.softmax(logits, axis=-1)
        return jnp.einsum('bhqk,bkhd->bqhd', probs.astype(v.dtype), v)

    out = lax.map(one_block, (to_blocks(q_nope), to_blocks(q_rope), q_pos))
    return jnp.moveaxis(out, 0, 1).reshape(b, s, MLA_HEADS * MLA_V)


def s5_mixer(u, a_re, a_im, log_dt, b_re, b_im, c_re, c_im, d_skip, glu_w, glu_b):
    b, s, _ = u.shape
    u32 = u.astype(F32).reshape(b, s, SSM_GROUPS, SSM_GROUP_IN)
    lam = lax.complex(a_re.astype(F32), a_im.astype(F32))
    dt = jnp.exp(log_dt.astype(F32))[:, None]
    lam_bar = jnp.exp(lam * dt)
    b_bar = ((lam_bar - 1.0) / lam)[..., None] * lax.complex(b_re.astype(F32), b_im.astype(F32))
    bu = jnp.einsum('bsgi,gpi->bsgp', u32.astype(jnp.complex64), b_bar)
    a_seq = jnp.broadcast_to(lam_bar, bu.shape)

    def combine(left, right):
        a_l, x_l = left
        a_r, x_r = right
        return a_r * a_l, a_r * x_l + x_r

    _, states = lax.associative_scan(combine, (a_seq, bu), axis=1)
    c = lax.complex(c_re.astype(F32), c_im.astype(F32))
    y = (jnp.einsum('gop,bsgp->bsgo', c, states).real
         + d_skip.astype(F32).reshape(SSM_GROUPS, SSM_GROUP_IN) * u32)
    y = jax.nn.gelu(y.reshape(b, s, SSM_CH))
    out = y * jax.nn.sigmoid(y @ glu_w.astype(F32) + glu_b.astype(F32))
    return out.astype(u.dtype)


def memory_cross_attention(h, mem_n, wq, wk, wv, qn_g, kn_g, wo):
    b, s, _ = h.shape
    m = mem_n.shape[1]
    q = rms_norm((h @ wq).reshape(b, s, XA_HEADS, XA_HEAD_DIM), qn_g)
    k = rms_norm((mem_n @ wk).reshape(b, m, XA_HEADS, XA_HEAD_DIM), kn_g)
    v = (mem_n @ wv).reshape(b, m, XA_HEADS, XA_HEAD_DIM)
    logits = jnp.einsum('bshd,bmhd->bhsm', q, k).astype(F32) * (XA_HEAD_DIM ** -0.5)
    probs = jax.nn.softmax(logits, axis=-1)
    o = jnp.einsum('bhsm,bmhd->bshd', probs.astype(v.dtype), v).reshape(b, s, XA_HEADS * XA_HEAD_DIM)
    return o @ wo


def dispatch_experts(ht, expert_flat, w_flat, w_gate, w_up, w_down):
    n_tok, d = ht.shape
    n_assign = expert_flat.shape[0]
    n_blocks = -(-(n_assign + N_EXPERTS * (EXPERT_BLOCK - 1)) // EXPERT_BLOCK)
    padded_len = n_blocks * EXPERT_BLOCK
    tok_flat = jnp.repeat(jnp.arange(n_tok, dtype=jnp.int32), EXPERT_TOPK)
    order = jnp.argsort(expert_flat)
    e_sorted = expert_flat[order]
    tok_sorted = tok_flat[order]
    w_sorted = w_flat[order]
    counts = jnp.bincount(expert_flat, length=N_EXPERTS)
    starts = jnp.cumsum(counts) - counts
    padded = (counts + EXPERT_BLOCK - 1) // EXPERT_BLOCK * EXPERT_BLOCK
    pad_ends = jnp.cumsum(padded)
    pad_starts = pad_ends - padded
    dest = pad_starts[e_sorted] + jnp.arange(n_assign, dtype=jnp.int32) - starts[e_sorted]
    tok_buf = jnp.full((padded_len,), n_tok, jnp.int32).at[dest].set(tok_sorted)
    w_buf = jnp.zeros((padded_len,), ht.dtype).at[dest].set(w_sorted.astype(ht.dtype))
    block_starts = jnp.arange(n_blocks, dtype=jnp.int32) * EXPERT_BLOCK
    block_expert = jnp.minimum(jnp.searchsorted(pad_ends, block_starts, side='right'),
                               N_EXPERTS - 1).astype(jnp.int32)
    x_pad = jnp.concatenate([ht, jnp.zeros((1, d), ht.dtype)], axis=0)
    xb = x_pad[tok_buf].reshape(n_blocks, EXPERT_BLOCK, d)

    def expert_block(args):
        xblk, e = args
        hid = jax.nn.silu(xblk @ w_gate[e]) * (xblk @ w_up[e])
        return hid @ w_down[e]

    out = lax.map(expert_block, (xb, block_expert)).reshape(padded_len, d)
    y = jnp.zeros((n_tok + 1, d), ht.dtype).at[tok_buf].add(out * w_buf[:, None])
    return y[:n_tok]


def hier_moe(h, w_group, b_group, w_expert, b_expert, w_gate, w_up, w_down):
    b, s, d = h.shape
    n_tok = b * s
    ht = h.reshape(n_tok, d)
    tok = jnp.arange(n_tok)
    g_logits = (ht @ w_group).astype(F32) + b_group.astype(F32)
    g_idx = jnp.argmax(g_logits, axis=-1)
    g_w = jax.nn.softmax(g_logits, axis=-1)[tok, g_idx]
    e_logits = ((ht @ w_expert).astype(F32) + b_expert.astype(F32)).reshape(
        n_tok, N_EXPERT_GROUPS, EXPERTS_PER_GROUP)[tok, g_idx]
    top_val, top_idx = lax.top_k(e_logits, EXPERT_TOPK)
    gate = jax.nn.softmax(top_val, axis=-1) * g_w[:, None]
    expert_id = (g_idx[:, None] * EXPERTS_PER_GROUP + top_idx).astype(jnp.int32)
    y = dispatch_experts(ht, expert_id.reshape(-1), gate.reshape(-1), w_gate, w_up, w_down)
    return y.reshape(b, s, d)


def setup_inputs(seed: int = 0) -> dict:
    key = jax.random.key(seed)
    keys = list(jax.random.split(key, 64))
    L = DEPTH

    def nk():
        return keys.pop()

    def nrm(shape, scale):
        return scale * jax.random.normal(nk(), shape, F32)

    def gain(shape):
        return 1.0 + 0.02 * jax.random.normal(nk(), shape, F32)

    a_im = jnp.pi * jnp.arange(SSM_STATE, dtype=F32)[None, None, :] + nrm((L, SSM_GROUPS, SSM_STATE), 0.01)
    return {
        'x': nrm((BATCH, SEQ, D_MODEL), 1.0),
        'mem': nrm((BATCH, MEM_TOKENS, D_MODEL), 1.0),
        'rel_bias': nrm((REL_BUCKETS, SWA_HEADS), 0.5),
        'ln_mix_g': gain((L, D_MODEL)),
        'w_in': nrm((L, D_MODEL, IN_WIDTH), D_MODEL ** -0.5),
        'swa_qn_g': gain((L, SWA_HEAD_DIM)),
        'swa_kn_g': gain((L, SWA_HEAD_DIM)),
        'swa_sinks': nrm((L, SWA_HEADS), 0.5),
        'conv_dw_w': nrm((L, CONV_WIDTH, CONV_CH), CONV_WIDTH ** -0.5),
        'conv_dw_b': nrm((L, CONV_CH), 0.02),
        'conv_ln_g': gain((L, CONV_CH)),
        'conv_ln_b': nrm((L, CONV_CH), 0.02),
        'conv_pw_w': nrm((L, CONV_CH, CONV_CH), CONV_CH ** -0.5),
        'conv_pw_b': nrm((L, CONV_CH), 0.02),
        'mla_qa_g': gain((L, MLA_Q_LORA)),
        'mla_kva_g': gain((L, MLA_KV_LORA)),
        'mla_wq_b': nrm((L, MLA_Q_LORA, MLA_HEADS * (MLA_NOPE + MLA_ROPE)), MLA_Q_LORA ** -0.5),
        'mla_wkv_b': nrm((L, MLA_KV_LORA, MLA_HEADS * (MLA_NOPE + MLA_V)), MLA_KV_LORA ** -0.5),
        'mla_qn_nope_g': gain((L, MLA_NOPE)),
        'mla_qn_rope_g': gain((L, MLA_ROPE)),
        'mla_kn_nope_g': gain((L, MLA_NOPE)),
        'mla_kn_rope_g': gain((L, MLA_ROPE)),
        'ssm_a_re': -0.5 + nrm((L, SSM_GROUPS, SSM_STATE), 0.01),
        'ssm_a_im': a_im,
        'ssm_log_dt': jax.random.uniform(nk(), (L, SSM_GROUPS), F32, math.log(DT_MIN), math.log(DT_MAX)),
        'ssm_b_re': nrm((L, SSM_GROUPS, SSM_STATE, SSM_GROUP_IN), (2 * SSM_GROUP_IN) ** -0.5),
        'ssm_b_im': nrm((L, SSM_GROUPS, SSM_STATE, SSM_GROUP_IN), (2 * SSM_GROUP_IN) ** -0.5),
        'ssm_c_re': nrm((L, SSM_GROUPS, SSM_GROUP_IN, SSM_STATE), SSM_STATE ** -0.5),
        'ssm_c_im': nrm((L, SSM_GROUPS, SSM_GROUP_IN, SSM_STATE), SSM_STATE ** -0.5),
        'ssm_d': nrm((L, SSM_CH), 1.0),
        'ssm_glu_w': nrm((L, SSM_CH, SSM_CH), SSM_CH ** -0.5),
        'ssm_glu_b': nrm((L, SSM_CH), 0.02),
        'mix_out_g': gain((L, MIX_WIDTH)),
        'w_out': nrm((L, MIX_WIDTH, D_MODEL), MIX_WIDTH ** -0.5),
        'ln_cross_g': gain((L, D_MODEL)),
        'mem_ln_g': gain((L, D_MODEL)),
        'xa_wq': nrm((L, D_MODEL, XA_HEADS * XA_HEAD_DIM), D_MODEL ** -0.5),
        'xa_wk': nrm((L, D_MODEL, XA_HEADS * XA_HEAD_DIM), D_MODEL ** -0.5),
        'xa_wv': nrm((L, D_MODEL, XA_HEADS * XA_HEAD_DIM), D_MODEL ** -0.5),
        'xa_qn_g': gain((L, XA_HEAD_DIM)),
        'xa_kn_g': gain((L, XA_HEAD_DIM)),
        'xa_wo': nrm((L, XA_HEADS * XA_HEAD_DIM, D_MODEL), (XA_HEADS * XA_HEAD_DIM) ** -0.5),
        'ln_moe_g': gain((L, D_MODEL)),
        'moe_w_group': nrm((L, D_MODEL, N_EXPERT_GROUPS), D_MODEL ** -0.5),
        'moe_b_group': nrm((L, N_EXPERT_GROUPS), 0.01),
        'moe_w_expert': nrm((L, D_MODEL, N_EXPERTS), D_MODEL ** -0.5),
        'moe_b_expert': nrm((L, N_EXPERTS), 0.01),
        'moe_w_gate': nrm((L, N_EXPERTS, D_MODEL, EXPERT_FF), D_MODEL ** -0.5),
        'moe_w_up': nrm((L, N_EXPERTS, D_MODEL, EXPERT_FF), D_MODEL ** -0.5),
        'moe_w_down': nrm((L, N_EXPERTS, EXPERT_FF, D_MODEL), EXPERT_FF ** -0.5),
    }


def reference(x, mem, rel_bias, ln_mix_g, w_in, swa_qn_g, swa_kn_g, swa_sinks,
              conv_dw_w, conv_dw_b, conv_ln_g, conv_ln_b, conv_pw_w, conv_pw_b,
              mla_qa_g, mla_kva_g, mla_wq_b, mla_wkv_b, mla_qn_nope_g, mla_qn_rope_g,
              mla_kn_nope_g, mla_kn_rope_g,
              ssm_a_re, ssm_a_im, ssm_log_dt, ssm_b_re, ssm_b_im, ssm_c_re, ssm_c_im,
              ssm_d, ssm_glu_w, ssm_glu_b,
              mix_out_g, w_out,
              ln_cross_g, mem_ln_g, xa_wq, xa_wk, xa_wv, xa_qn_g, xa_kn_g, xa_wo,
              ln_moe_g, moe_w_group, moe_b_group, moe_w_expert, moe_b_expert,
              moe_w_gate, moe_w_up, moe_w_down):
    b, s, _ = x.shape
    cos, sin = rope_tables(s)
    split_at = [int(i) for i in np.cumsum(IN_SPLITS)[:-1]]
    for l in range(DEPTH):
        h = rms_norm(x, ln_mix_g[l])
        a_q, a_k, a_v, b_a, b_g, c_q, c_kv, c_kr, d_u = jnp.split(h @ w_in[l], split_at, axis=-1)
        q = rms_norm(a_q.reshape(b, s, SWA_HEADS, SWA_HEAD_DIM), swa_qn_g[l])
        k = rms_norm(a_k.reshape(b, s, SWA_KV_HEADS, SWA_HEAD_DIM), swa_kn_g[l])
        v = a_v.reshape(b, s, SWA_KV_HEADS, SWA_HEAD_DIM)
        o_a = swa_sink_attention(q, k, v, swa_sinks[l], rel_bias)
        o_b = conformer_conv(b_a, b_g, conv_dw_w[l], conv_dw_b[l], conv_ln_g[l], conv_ln_b[l],
                             conv_pw_w[l], conv_pw_b[l])
        o_c = mla_attention(c_q, c_kv, c_kr, mla_qa_g[l], mla_kva_g[l], mla_wq_b[l], mla_wkv_b[l],
                            mla_qn_nope_g[l], mla_qn_rope_g[l], mla_kn_nope_g[l], mla_kn_rope_g[l],
                            cos, sin)
        o_d = s5_mixer(d_u, ssm_a_re[l], ssm_a_im[l], ssm_log_dt[l], ssm_b_re[l], ssm_b_im[l],
                       ssm_c_re[l], ssm_c_im[l], ssm_d[l], ssm_glu_w[l], ssm_glu_b[l])
        o = jnp.concatenate([o_a, o_b, o_c, o_d], axis=-1).reshape(b, s, N_MIXERS, GROUP_WIDTH)
        o = rms_norm(o, mix_out_g[l].reshape(N_MIXERS, GROUP_WIDTH)).reshape(b, s, MIX_WIDTH)
        x = x + o @ w_out[l]
        x = x + memory_cross_attention(rms_norm(x, ln_cross_g[l]), rms_norm(mem, mem_ln_g[l]),
                                       xa_wq[l], xa_wk[l], xa_wv[l], xa_qn_g[l], xa_kn_g[l], xa_wo[l])
        x = x + hier_moe(rms_norm(x, ln_moe_g[l]), moe_w_group[l], moe_b_group[l],
                         moe_w_expert[l], moe_b_expert[l], moe_w_gate[l], moe_w_up[l], moe_w_down[l])
    return x
```

```python
import functools
import math

import jax
import jax.numpy as jnp
import numpy as np
from jax import lax
from jax.experimental import pallas as pl
from jax.experimental.pallas import tpu as pltpu

F32 = jnp.float32
BF16 = jnp.bfloat16

D_MODEL = 2048
DEPTH = 4
GROUP_WIDTH = 512
SWA_HEADS = 8
SWA_KV_HEADS = 2
SWA_HEAD_DIM = 64
SWA_BLOCK = 128
CONV_WIDTH = 31
MLA_HEADS = 4
MLA_NOPE = 128
MLA_ROPE = 64
MLA_V = 128
MLA_QK_PAD = 256
ROPE_THETA = 10000.0
SSM_GROUPS = 32
SSM_GROUP_IN = 16
SSM_STATE = 64
SSM_CHUNK = 64
REL_BUCKETS = 32
REL_MAX_DIST = 128
XA_HEADS = 4
XA_HEAD_DIM = 128
N_EXPERT_GROUPS = 4
EXPERTS_PER_GROUP = 8
N_EXPERTS = 32
EXPERT_FF = 512
EXPERT_BLOCK = 256
NORM_EPS = 1e-6
NEG_INF = -1e30

PA_W = 768
PB_W = 1024
PC_W = 896
PD_W = 512
P_SEGS = (PA_W, PB_W, PC_W, PD_W)

V7X_VMEM_LIMIT = 48 * 1024 * 1024

NT_DIMS = (((1,), (1,)), ((), ()))


def _cparams(sem, vmem=V7X_VMEM_LIMIT):
    return pltpu.CompilerParams(dimension_semantics=sem, vmem_limit_bytes=vmem)


def _const_spec(shape):
    nd = len(shape)
    return pl.BlockSpec(shape, lambda *_: (0,) * nd, pipeline_mode=pl.Buffered(1))


def _rms(x, g):
    return x * lax.rsqrt(jnp.mean(x * x, axis=-1, keepdims=True) + NORM_EPS) * g


def _dot(a, b):
    return jnp.dot(a, b, preferred_element_type=F32)


def _dot_nt(a, b):
    return lax.dot_general(a, b, NT_DIMS, preferred_element_type=F32)


def _norm_proj_kernel(x_ref, g_ref, w_ref, *o_refs, segs):
    h = _rms(x_ref[...], g_ref[...]).astype(BF16)
    off = 0
    for o_ref, n in zip(o_refs, segs):
        o_ref[...] = _dot(h, w_ref[:, off:off + n])
        off += n


def _norm_proj(x, g, w, segs, tm):
    s, d = x.shape
    n = sum(segs)
    return pl.pallas_call(
        functools.partial(_norm_proj_kernel, segs=segs),
        out_shape=[jax.ShapeDtypeStruct((s, n_i), F32) for n_i in segs],
        grid=(s // tm,),
        in_specs=[pl.BlockSpec((tm, d), lambda i: (i, 0)), _const_spec((1, d)), _const_spec((d, n))],
        out_specs=[pl.BlockSpec((tm, n_i), lambda i: (i, 0)) for n_i in segs],
        compiler_params=_cparams(("parallel",)),
        name="norm_proj",
    )(x, g.reshape(1, d), w)


def _t5_buckets():
    qi = np.arange(SWA_BLOCK)[:, None]
    kj = np.arange(2 * SWA_BLOCK)[None, :]
    dist = SWA_BLOCK + qi - kj
    valid = (dist >= 0) & (dist < SWA_BLOCK)
    max_exact = REL_BUCKETS // 2
    d0 = np.maximum(dist, 0)
    dl = np.maximum(dist, 1).astype(np.float32)
    large = max_exact + (np.log(dl / max_exact) / np.log(REL_MAX_DIST / max_exact)
                         * (REL_BUCKETS - max_exact)).astype(np.int32)
    large = np.minimum(large, REL_BUCKETS - 1)
    bucket = np.where(d0 < max_exact, d0, large).astype(np.int32)
    return bucket, valid.astype(np.int32)


def _bias_kernel(rb_ref, bucket_ref, valid_ref, o_ref):
    b = bucket_ref[...]
    ok = valid_ref[...] > 0
    for h in range(SWA_HEADS):
        acc = jnp.zeros(b.shape, F32)
        for k in range(REL_BUCKETS):
            acc = jnp.where(b == k, rb_ref[k, h], acc)
        o_ref[h] = jnp.where(ok, acc, NEG_INF)


def _bias_table(rel_bias):
    bucket, valid = _t5_buckets()
    tab = pl.pallas_call(
        _bias_kernel,
        out_shape=jax.ShapeDtypeStruct((SWA_HEADS, SWA_BLOCK, 2 * SWA_BLOCK), F32),
        in_specs=[pl.BlockSpec(memory_space=pltpu.SMEM),
                  pl.BlockSpec(memory_space=pltpu.VMEM), pl.BlockSpec(memory_space=pltpu.VMEM)],
        out_specs=pl.BlockSpec(memory_space=pltpu.VMEM),
        name="t5_bias_table",
    )(rel_bias, jnp.asarray(bucket), jnp.asarray(valid))
    grp = SWA_HEADS // SWA_KV_HEADS
    return tab.reshape(SWA_KV_HEADS, grp * SWA_BLOCK, 2 * SWA_BLOCK)


SWA_QB = 4


def _swa_kernel(sink_ref, cur_ref, prev_ref, qg_ref, kg_ref, bias_ref, o_ref):
    i = pl.program_id(0)
    blk, dh = SWA_BLOCK, SWA_HEAD_DIM
    grp = SWA_HEADS // SWA_KV_HEADS
    qg = qg_ref[...]
    kg = kg_ref[...]
    scale = dh ** -0.5
    kv_all = jnp.concatenate([prev_ref[:, 512:768], cur_ref[:, 512:768]], axis=0)
    col = lax.broadcasted_iota(jnp.int32, (grp * blk, 2 * blk), 1)
    for kh in range(SWA_KV_HEADS):
        kn = _rms(kv_all[:, kh * dh:(kh + 1) * dh], kg).astype(BF16)
        vv = kv_all[:, 128 + kh * dh:128 + (kh + 1) * dh].astype(BF16)
        sink = jnp.concatenate([jnp.full((blk, 1), sink_ref[kh * grp + g], F32) for g in range(grp)], axis=0)
        for j in range(SWA_QB):
            rows = slice(j * blk, (j + 1) * blk)
            qs = jnp.concatenate(
                [_rms(cur_ref[rows, (kh * grp + g) * dh:(kh * grp + g + 1) * dh], qg) for g in range(grp)],
                axis=0)
            kb = kn[j * blk:(j + 2) * blk]
            vb = vv[j * blk:(j + 2) * blk]
            logits = _dot_nt((qs * scale).astype(BF16), kb) + bias_ref[kh]
            if j == 0:
                logits = jnp.where(jnp.logical_and(i == 0, col < blk), NEG_INF, logits)
            m = jnp.maximum(jnp.max(logits, axis=-1, keepdims=True), sink)
            p = jnp.exp(logits - m)
            denom = jnp.sum(p, axis=-1, keepdims=True) + jnp.exp(sink - m)
            out = _dot(p.astype(BF16), vb) / denom
            for g in range(grp):
                hq = kh * grp + g
                o_ref[rows, hq * dh:(hq + 1) * dh] = out[g * blk:(g + 1) * blk]


def _swa(pa, qn_g, kn_g, sinks, bias_tab):
    s = pa.shape[0]
    tq = SWA_QB * SWA_BLOCK
    grp = SWA_HEADS // SWA_KV_HEADS
    return pl.pallas_call(
        _swa_kernel,
        out_shape=jax.ShapeDtypeStruct((s, GROUP_WIDTH), F32),
        grid_spec=pltpu.PrefetchScalarGridSpec(
            num_scalar_prefetch=1,
            grid=(s // tq,),
            in_specs=[pl.BlockSpec((tq, PA_W), lambda i, sk: (i, 0)),
                      pl.BlockSpec((SWA_BLOCK, PA_W), lambda i, sk: (jnp.maximum(i * SWA_QB - 1, 0), 0)),
                      _const_spec((1, SWA_HEAD_DIM)), _const_spec((1, SWA_HEAD_DIM)),
                      _const_spec((SWA_KV_HEADS, grp * SWA_BLOCK, 2 * SWA_BLOCK))],
            out_specs=pl.BlockSpec((tq, GROUP_WIDTH), lambda i, sk: (i, 0)),
        ),
        compiler_params=_cparams(("parallel",)),
        name="swa_attention",
    )(sinks, pa, pa, qn_g.reshape(1, -1), kn_g.reshape(1, -1), bias_tab)


CONV_TILE = 256
CONV_HALO = 32


def _conv_kernel(cur_ref, prev_ref, dww_ref, dwb_ref, lng_ref, lnb_ref, pww_ref, pwb_ref, o_ref, z_scr):
    i = pl.program_id(0)
    t, c = CONV_TILE, GROUP_WIDTH
    z_scr[CONV_HALO:CONV_HALO + t, :] = cur_ref[:, 0:c] * jax.nn.sigmoid(cur_ref[:, c:2 * c])
    zp = prev_ref[:, 0:c] * jax.nn.sigmoid(prev_ref[:, c:2 * c])
    z_scr[0:CONV_HALO, :] = jnp.where(i > 0, zp, 0.0)
    base = CONV_HALO - (CONV_WIDTH - 1)
    acc = jnp.zeros((t, c), F32)
    for k in range(CONV_WIDTH):
        acc = acc + dww_ref[k:k + 1, :] * z_scr[base + k:base + k + t, :]
    acc = acc + dwb_ref[...]
    xc = acc - jnp.mean(acc, axis=-1, keepdims=True)
    y = xc * lax.rsqrt(jnp.mean(xc * xc, axis=-1, keepdims=True) + NORM_EPS) * lng_ref[...] + lnb_ref[...]
    y = y * jax.nn.sigmoid(y)
    o_ref[...] = _dot(y.astype(BF16), pww_ref[...]) + pwb_ref[...]


def _conv(pb, dw_w, dw_b, ln_g, ln_b, pw_w, pw_b):
    s = pb.shape[0]
    c = GROUP_WIDTH
    r = CONV_TILE // CONV_HALO
    row = lambda v: v.reshape(1, c)
    return pl.pallas_call(
        _conv_kernel,
        out_shape=jax.ShapeDtypeStruct((s, c), F32),
        grid=(s // CONV_TILE,),
        in_specs=[pl.BlockSpec((CONV_TILE, PB_W), lambda i: (i, 0)),
                  pl.BlockSpec((CONV_HALO, PB_W), lambda i: (jnp.maximum(i * r - 1, 0), 0)),
                  _const_spec((CONV_WIDTH, c)), _const_spec((1, c)), _const_spec((1, c)), _const_spec((1, c)),
                  _const_spec((c, c)), _const_spec((1, c))],
        out_specs=pl.BlockSpec((CONV_TILE, c), lambda i: (i, 0)),
        scratch_shapes=[pltpu.VMEM((CONV_TILE + CONV_HALO, c), F32)],
        compiler_params=_cparams(("parallel",)),
        name="conformer_conv",
    )(pb, pb, dw_w, row(dw_b), row(ln_g), row(ln_b), pw_w, row(pw_b))


MLA_TILE = 256


def _rot_half64(x):
    n = x.shape[-1]
    ax = x.ndim - 1
    lane = lax.broadcasted_iota(jnp.int32, x.shape, ax)
    return jnp.where((lane % 64) < 32, -pltpu.roll(x, n - 32, ax), pltpu.roll(x, 32, ax))


def _seg_rms64(x):
    n = x.shape[-1]
    lane = lax.broadcasted_iota(jnp.int32, x.shape, x.ndim - 1)
    x2 = x * x
    r = jnp.zeros(x.shape, F32)
    for h in range(n // 64):
        msk = jnp.logical_and(lane >= h * 64, lane < (h + 1) * 64)
        sm = jnp.sum(jnp.where(msk, x2, 0.0), axis=-1, keepdims=True)
        r = jnp.where(msk, lax.rsqrt(sm * (1.0 / 64) + NORM_EPS), r)
    return x * r


def _mla_proj_kernel(pc_ref, cos_ref, sin_ref, qag_ref, kvag_ref, wq_ref, wkv_ref,
                     qng_ref, qrg_ref, kng_ref, krg_ref, q_ref, k_ref, v_ref):
    h_n, dn = MLA_HEADS, MLA_NOPE
    scale = (MLA_NOPE + MLA_ROPE) ** -0.5
    cos4 = cos_ref[...]
    sin4 = sin_ref[...]
    qn = _rms(pc_ref[:, 0:512], qag_ref[...]).astype(BF16)
    q = _dot(qn, wq_ref[...])
    kvn = _rms(pc_ref[:, 512:768], kvag_ref[...]).astype(BF16)
    kv = _dot(kvn, wkv_ref[...])
    qr = _seg_rms64(q[:, h_n * dn:]) * qrg_ref[...]
    qr = (qr * cos4 + _rot_half64(qr) * sin4) * scale
    kr = pc_ref[:, 768:896]
    kr = kr * lax.rsqrt(jnp.sum(kr * kr, axis=-1, keepdims=True) * (1.0 / MLA_ROPE) + NORM_EPS) * krg_ref[...]
    kr = (kr * cos4[:, 0:128] + _rot_half64(kr) * sin4[:, 0:128]).astype(BF16)
    lane = lax.broadcasted_iota(jnp.int32, (q.shape[0], 128), 1)
    for h in range(h_n):
        qnope = _rms(q[:, h * dn:(h + 1) * dn], qng_ref[...]) * scale
        blk = qr[:, (h // 2) * 128:(h // 2 + 1) * 128]
        if h % 2 == 1:
            blk = pltpu.roll(blk, 64, 1)
        blk = jnp.where(lane < 64, blk, 0.0)
        q_ref[h] = jnp.concatenate([qnope, blk], axis=-1).astype(BF16)
        knope = _rms(kv[:, h * dn:(h + 1) * dn], kng_ref[...]).astype(BF16)
        k_ref[h] = jnp.concatenate([knope, kr], axis=-1)
        v_ref[h] = kv[:, (h_n + h) * dn:(h_n + h + 1) * dn].astype(BF16)


def _mla_proj(pc, cos4, sin4, qa_g, kva_g, wq, wkv, qn_nope_g, qn_rope_g4, kn_nope_g, kn_rope_g_pad):
    s = pc.shape[0]
    t = MLA_TILE
    h = MLA_HEADS
    row = lambda v: v.reshape(1, -1)
    return pl.pallas_call(
        _mla_proj_kernel,
        out_shape=[jax.ShapeDtypeStruct((h, s, MLA_QK_PAD), BF16),
                   jax.ShapeDtypeStruct((h, s, MLA_QK_PAD), BF16),
                   jax.ShapeDtypeStruct((h, s, MLA_V), BF16)],
        grid=(s // t,),
        in_specs=[pl.BlockSpec((t, PC_W), lambda i: (i, 0)),
                  pl.BlockSpec((t, 256), lambda i: (i, 0)), pl.BlockSpec((t, 256), lambda i: (i, 0)),
                  _const_spec((1, 512)), _const_spec((1, 256)),
                  _const_spec((512, 768)), _const_spec((256, 1024)),
                  _const_spec((1, 128)), _const_spec((1, 256)), _const_spec((1, 128)), _const_spec((1, 128))],
        out_specs=[pl.BlockSpec((h, t, MLA_QK_PAD), lambda i: (0, i, 0)),
                   pl.BlockSpec((h, t, MLA_QK_PAD), lambda i: (0, i, 0)),
                   pl.BlockSpec((h, t, MLA_V), lambda i: (0, i, 0))],
        compiler_params=_cparams(("parallel",)),
        name="mla_proj",
    )(pc, cos4, sin4, row(qa_g), row(kva_g), wq, wkv, row(qn_nope_g), row(qn_rope_g4), row(kn_nope_g),
      row(kn_rope_g_pad))


FLASH_BQ = 256
FLASH_BK = 512


def _flash_kernel(q_ref, k_ref, v_ref, o_ref, m_scr, l_scr, acc_scr):
    bq, bk = FLASH_BQ, FLASH_BK
    qi = pl.program_id(1)
    q = q_ref[...]
    m_scr[...] = jnp.full(m_scr.shape, NEG_INF, F32)
    l_scr[...] = jnp.zeros(l_scr.shape, F32)
    acc_scr[...] = jnp.zeros(acc_scr.shape, F32)
    n_full = (qi * bq) // bk

    def step(start, masked):
        k = k_ref[pl.ds(start, bk), :]
        v = v_ref[pl.ds(start, bk), :]
        s = _dot_nt(q, k)
        if masked:
            qpos = qi * bq + lax.broadcasted_iota(jnp.int32, (bq, bk), 0)
            kpos = start + lax.broadcasted_iota(jnp.int32, (bq, bk), 1)
            s = jnp.where(kpos <= qpos, s, NEG_INF)
        m_prev = m_scr[...]
        m_new = jnp.maximum(m_prev, jnp.max(s, axis=-1, keepdims=True))
        alpha = jnp.exp(m_prev - m_new)
        p = jnp.exp(s - m_new)
        l_scr[...] = alpha * l_scr[...] + jnp.sum(p, axis=-1, keepdims=True)
        acc_scr[...] = alpha * acc_scr[...] + _dot(p.astype(BF16), v)
        m_scr[...] = m_new

    def body(j, carry):
        step(pl.multiple_of(j * bk, bk), False)
        return carry

    lax.fori_loop(0, n_full, body, 0)
    step(pl.multiple_of(n_full * bk, bk), True)
    o_ref[...] = acc_scr[...] / l_scr[...]


def _flash(q, k, v):
    h, s, _ = q.shape
    return pl.pallas_call(
        _flash_kernel,
        out_shape=jax.ShapeDtypeStruct((s, h * MLA_V), F32),
        grid=(h, s // FLASH_BQ),
        in_specs=[pl.BlockSpec((None, FLASH_BQ, MLA_QK_PAD), lambda hh, i: (hh, i, 0)),
                  pl.BlockSpec((None, s, MLA_QK_PAD), lambda hh, i: (hh, 0, 0)),
                  pl.BlockSpec((None, s, MLA_V), lambda hh, i: (hh, 0, 0))],
        out_specs=pl.BlockSpec((FLASH_BQ, MLA_V), lambda hh, i: (i, hh)),
        scratch_shapes=[pltpu.VMEM((FLASH_BQ, 1), F32), pltpu.VMEM((FLASH_BQ, 1), F32),
                        pltpu.VMEM((FLASH_BQ, MLA_V), F32)],
        compiler_params=_cparams(("parallel", "arbitrary")),
        name="mla_flash",
    )(q, k, v)


def _split3_dot(a, sel):
    a1 = a.astype(BF16)
    r1 = a - a1.astype(F32)
    a2 = r1.astype(BF16)
    a3 = (r1 - a2.astype(F32)).astype(BF16)
    return _dot(a1, sel) + _dot(a2, sel) + _dot(a3, sel)


def _hi_dot(a, b):
    return lax.dot_general(a, b, (((1,), (0,)), ((), ())), precision=lax.Precision.HIGHEST,
                           preferred_element_type=F32)


def _ssm_kernel(u_ref, arc_ref, aic_ref, arr_ref, air_ref, ldt_ref, bre_ref, bim_ref, btre_ref, btim_ref,
                ctre_ref, ctim_ref, dcol_ref, y_ref, toep_scr, *, nch):
    tc, n_in, n_st = SSM_CHUNK, SSM_GROUP_IN, SSM_STATE
    w = tc * n_in
    dt = jnp.exp(ldt_ref[...])
    ar_c, ai_c = arc_ref[...], aic_ref[...]
    ar_r, ai_r = arr_ref[...], air_ref[...]

    def lam_pow(ar, ai, e):
        mag = jnp.exp(ar * dt * e)
        ang = ai * dt * e
        return mag * jnp.cos(ang), mag * jnp.sin(ang)

    def zoh_coef(ar, ai):
        lr, li = lam_pow(ar, ai, 1.0)
        x, y = lr - 1.0, li
        den = ar * ar + ai * ai
        return (x * ar + y * ai) / den, (y * ar - x * ai) / den

    lane_w = lax.broadcasted_iota(jnp.int32, (tc, w), 1)
    sel_tau = (lane_w // n_in == lax.broadcasted_iota(jnp.int32, (tc, w), 0)).astype(BF16)
    lane_o = lax.broadcasted_iota(jnp.int32, (n_in, w), 1)
    sel_o = (lane_o % n_in == lax.broadcasted_iota(jnp.int32, (n_in, w), 0)).astype(BF16)

    tau = lax.broadcasted_iota(jnp.int32, (n_st, tc), 1).astype(F32)
    lre, lim = lam_pow(ar_c, ai_c, tau)
    lre_x, lim_x = _split3_dot(lre, sel_tau), _split3_dot(lim, sel_tau)
    rre, rim = lam_pow(ar_c, ai_c, (tc - 1.0) - tau)
    rre_x, rim_x = _split3_dot(rre, sel_tau), _split3_dot(rim, sel_tau)
    ctre_x, ctim_x = _split3_dot(ctre_ref[...], sel_o), _split3_dot(ctim_ref[...], sel_o)

    e_re = lre_x * ctre_x - lim_x * ctim_x
    e_im = lre_x * ctim_x + lim_x * ctre_x

    cr_c, ci_c = zoh_coef(ar_c, ai_c)
    bb_re = cr_c * bre_ref[...] - ci_c * bim_ref[...]
    bb_im = cr_c * bim_ref[...] + ci_c * bre_ref[...]
    bb_re_x, bb_im_x = _split3_dot(bb_re, sel_o), _split3_dot(bb_im, sel_o)
    f_re = (rre_x * bb_re_x - rim_x * bb_im_x).astype(BF16)
    f_im = (rre_x * bb_im_x + rim_x * bb_re_x).astype(BF16)

    cr_r, ci_r = zoh_coef(ar_r, ai_r)
    bbt_re = cr_r * btre_ref[...] - ci_r * btim_ref[...]
    bbt_im = cr_r * btim_ref[...] + ci_r * btre_ref[...]
    g_row = _hi_dot(bbt_re, e_re) - _hi_dot(bbt_im, e_im)
    lane_g = lax.broadcasted_iota(jnp.int32, (n_in, w), 1)
    row_g = lax.broadcasted_iota(jnp.int32, (n_in, w), 0)
    g_row = g_row + jnp.where(lane_g == row_g, dcol_ref[...], 0.0)
    for s in range(tc):
        rolled = g_row if s == 0 else pltpu.roll(g_row, n_in * s, 1)
        toep_scr[s * n_in:(s + 1) * n_in, :] = jnp.where(lane_g >= n_in * s, rolled, 0.0).astype(BF16)

    u = u_ref[...]
    s_re = _dot_nt(u, f_re)
    s_im = _dot_nt(u, f_im)
    row = lax.broadcasted_iota(jnp.int32, (nch, n_st), 0)
    sh = 1
    while sh < nch:
        pr, pi = lam_pow(ar_r, ai_r, float(tc * sh))
        re_s = jnp.where(row >= sh, pltpu.roll(s_re, sh, 0), 0.0)
        im_s = jnp.where(row >= sh, pltpu.roll(s_im, sh, 0), 0.0)
        s_re, s_im = s_re + pr * re_s - pi * im_s, s_im + pr * im_s + pi * re_s
        sh *= 2
    p_re = jnp.where(row >= 1, pltpu.roll(s_re, 1, 0), 0.0)
    p_im = jnp.where(row >= 1, pltpu.roll(s_im, 1, 0), 0.0)
    l1r, l1i = lam_pow(ar_c, ai_c, 1.0)
    h_re = (l1r * e_re - l1i * e_im).astype(BF16)
    h_im = (l1r * e_im + l1i * e_re).astype(BF16)
    y = _dot(u, toep_scr[...]) + _dot(p_re.astype(BF16), h_re) - _dot(p_im.astype(BF16), h_im)
    y_ref[...] = y


def _ssm(u_r, a_re, a_im, log_dt, b_re, b_im, c_re, c_im, d_skip):
    g, nch, w = u_r.shape
    n_st, n_in = SSM_STATE, SSM_GROUP_IN
    col = lambda v: v.reshape(g, n_st, 1)
    rw = lambda v: v.reshape(g, 1, n_st)
    spec = lambda *shape: pl.BlockSpec((None,) + shape, lambda i: (i,) + (0,) * len(shape))
    return pl.pallas_call(
        functools.partial(_ssm_kernel, nch=nch),
        out_shape=jax.ShapeDtypeStruct((g, nch, w), F32),
        grid=(g,),
        in_specs=[spec(nch, w), spec(n_st, 1), spec(n_st, 1), spec(1, n_st), spec(1, n_st), spec(1, 1),
                  spec(n_st, n_in), spec(n_st, n_in), spec(n_in, n_st), spec(n_in, n_st),
                  spec(n_st, n_in), spec(n_st, n_in), spec(n_in, 1)],
        out_specs=spec(nch, w),
        scratch_shapes=[pltpu.VMEM((w, w), BF16)],
        compiler_params=_cparams(("parallel",)),
        name="s5_ssm",
    )(u_r, col(a_re), col(a_im), rw(a_re), rw(a_im), log_dt.reshape(g, 1, 1), b_re, b_im,
      jnp.swapaxes(b_re, 1, 2), jnp.swapaxes(b_im, 1, 2), jnp.swapaxes(c_re, 1, 2), jnp.swapaxes(c_im, 1, 2),
      d_skip.reshape(g, n_in, 1))


OUT_TILE = 256


def _mix_out_kernel(x_ref, oa_ref, ob_ref, oc_ref, yd_ref, gw_ref, gb_ref, mg_ref, wo_ref, o_ref):
    c = GROUP_WIDTH
    y = jax.nn.gelu(yd_ref[...])
    od = y * jax.nn.sigmoid(_dot(y.astype(BF16), gw_ref[...]) + gb_ref[...])
    acc = x_ref[...]
    for m, o in enumerate((oa_ref[...], ob_ref[...], oc_ref[...], od)):
        n = _rms(o, mg_ref[:, m * c:(m + 1) * c]).astype(BF16)
        acc = acc + _dot(n, wo_ref[m * c:(m + 1) * c, :])
    o_ref[...] = acc


def _mix_out(x, oa, ob, oc, yd, glu_w, glu_b, mix_g, w_out):
    s, d = x.shape
    c = GROUP_WIDTH
    t = OUT_TILE
    tile = lambda n: pl.BlockSpec((t, n), lambda i: (i, 0))
    return pl.pallas_call(
        _mix_out_kernel,
        out_shape=jax.ShapeDtypeStruct((s, d), F32),
        grid=(s // t,),
        in_specs=[tile(d), tile(c), tile(c), tile(c), tile(c),
                  _const_spec((c, c)), _const_spec((1, c)), _const_spec((1, d)), _const_spec((d, d))],
        out_specs=tile(d),
        compiler_params=_cparams(("parallel",)),
        name="mix_out",
    )(x, oa, ob, oc, yd, glu_w, glu_b.reshape(1, c), mix_g.reshape(1, d), w_out)


XA_TILE = 256


def _xattn_kernel(x_ref, lg_ref, wq_ref, qg_ref, kv_ref, kg_ref, wo_ref, o_ref):
    dh = XA_HEAD_DIM
    hw = XA_HEADS * dh
    x = x_ref[...]
    q = _dot(_rms(x, lg_ref[...]).astype(BF16), wq_ref[...])
    scale = dh ** -0.5
    acc = x
    for h in range(XA_HEADS):
        qh = (_rms(q[:, h * dh:(h + 1) * dh], qg_ref[...]) * scale).astype(BF16)
        kh = _rms(kv_ref[:, h * dh:(h + 1) * dh], kg_ref[...]).astype(BF16)
        vh = kv_ref[:, hw + h * dh:hw + (h + 1) * dh].astype(BF16)
        logits = _dot_nt(qh, kh)
        p = jnp.exp(logits - jnp.max(logits, axis=-1, keepdims=True))
        oh = _dot(p.astype(BF16), vh) / jnp.sum(p, axis=-1, keepdims=True)
        acc = acc + _dot(oh.astype(BF16), wo_ref[h * dh:(h + 1) * dh, :])
    o_ref[...] = acc


def _xattn(x, ln_g, wq, qn_g, kv, kn_g, wo):
    s, d = x.shape
    m, kvw = kv.shape
    hw = XA_HEADS * XA_HEAD_DIM
    t = XA_TILE
    return pl.pallas_call(
        _xattn_kernel,
        out_shape=jax.ShapeDtypeStruct((s, d), F32),
        grid=(s // t,),
        in_specs=[pl.BlockSpec((t, d), lambda i: (i, 0)), _const_spec((1, d)), _const_spec((d, hw)),
                  _const_spec((1, XA_HEAD_DIM)), _const_spec((m, kvw)), _const_spec((1, XA_HEAD_DIM)),
                  _const_spec((hw, d))],
        out_specs=pl.BlockSpec((t, d), lambda i: (i, 0)),
        compiler_params=_cparams(("parallel",)),
        name="mem_xattn",
    )(x, ln_g.reshape(1, d), wq, qn_g.reshape(1, -1), kv, kn_g.reshape(1, -1), wo)


ROUTER_TILE = 256
ROUTER_LANES = 128


def _router_kernel(x_ref, lg_ref, whi_ref, wlo_ref, b_ref, h_ref, r_ref, cnt_ref, carry_scr):
    i = pl.program_id(0)
    t = ROUTER_TILE
    ng, epg = N_EXPERT_GROUPS, EXPERTS_PER_GROUP

    @pl.when(i == 0)
    def _():
        carry_scr[...] = jnp.zeros(carry_scr.shape, F32)

    h = _rms(x_ref[...], lg_ref[...])
    h_hi = h.astype(BF16)
    h_lo = (h - h_hi.astype(F32)).astype(BF16)
    h_ref[...] = h_hi
    logits = _dot(h_hi, whi_ref[...]) + _dot(h_hi, wlo_ref[...]) + _dot(h_lo, whi_ref[...]) + b_ref[...]
    lane = lax.broadcasted_iota(jnp.int32, (t, ROUTER_LANES), 1)
    big = jnp.int32(ROUTER_LANES)

    def first_max(vals, mask):
        v = jnp.where(mask, vals, -jnp.inf)
        mx = jnp.max(v, axis=-1, keepdims=True)
        idx = jnp.min(jnp.where(jnp.logical_and(mask, v == mx), lane, big), axis=-1, keepdims=True)
        return mx, idx

    gmask = lane < ng
    gmax, gidx = first_max(logits, gmask)
    g_w = 1.0 / jnp.sum(jnp.where(gmask, jnp.exp(logits - gmax), 0.0), axis=-1, keepdims=True)
    emask = jnp.logical_and(lane >= ng + gidx * epg, lane < ng + (gidx + 1) * epg)
    v1, i1 = first_max(logits, emask)
    v2, i2 = first_max(logits, jnp.logical_and(emask, lane != i1))
    e2 = jnp.exp(v2 - v1)
    gate1 = g_w / (1.0 + e2)
    gate2 = g_w * e2 / (1.0 + e2)

    oh1 = lane == i1
    oh2 = lane == i2
    both = (oh1.astype(F32) + oh2.astype(F32))
    tri = (lax.broadcasted_iota(jnp.int32, (t, t), 1) < lax.broadcasted_iota(jnp.int32, (t, t), 0)).astype(BF16)
    before = _dot(tri, both.astype(BF16)) + carry_scr[...]
    rank1 = jnp.sum(jnp.where(oh1, before, 0.0), axis=-1, keepdims=True)
    rank2 = jnp.sum(jnp.where(oh2, before, 0.0), axis=-1, keepdims=True)
    carry_scr[...] = carry_scr[...] + jnp.sum(both, axis=0, keepdims=True)
    cnt_ref[...] = jnp.broadcast_to(carry_scr[...], cnt_ref.shape)

    out = jnp.zeros((t, ROUTER_LANES), F32)
    for k, val in enumerate(((i1 - ng).astype(F32), (i2 - ng).astype(F32), gate1, gate2, rank1, rank2)):
        out = jnp.where(lane == k, val, out)
    r_ref[...] = out


def _router(x, ln_g, w_hi, w_lo, bias):
    s, d = x.shape
    t = ROUTER_TILE
    return pl.pallas_call(
        _router_kernel,
        out_shape=[jax.ShapeDtypeStruct((s, d), BF16), jax.ShapeDtypeStruct((s, ROUTER_LANES), F32),
                   jax.ShapeDtypeStruct((8, ROUTER_LANES), F32)],
        grid=(s // t,),
        in_specs=[pl.BlockSpec((t, d), lambda i: (i, 0)), _const_spec((1, d)),
                  _const_spec((d, ROUTER_LANES)), _const_spec((d, ROUTER_LANES)), _const_spec((1, ROUTER_LANES))],
        out_specs=[pl.BlockSpec((t, d), lambda i: (i, 0)), pl.BlockSpec((t, ROUTER_LANES), lambda i: (i, 0)),
                   pl.BlockSpec((8, ROUTER_LANES), lambda i: (0, 0))],
        scratch_shapes=[pltpu.VMEM((1, ROUTER_LANES), F32)],
        compiler_params=_cparams(("arbitrary",)),
        name="moe_router",
    )(x, ln_g.reshape(1, d), w_hi, w_lo, bias)


def _ffn_kernel(be_ref, nu_ref, x_ref, wg_ref, wu_ref, wd_ref, o_ref):
    b = pl.program_id(0)

    @pl.when(b < nu_ref[0])
    def _():
        x = x_ref[...]
        g = _dot(x, wg_ref[...])
        hid = (g * jax.nn.sigmoid(g)) * _dot(x, wu_ref[...])
        o_ref[...] = _dot(hid.astype(BF16), wd_ref[...]).astype(o_ref.dtype)

    @pl.when(b >= nu_ref[0])
    def _():
        o_ref[...] = jnp.zeros(o_ref.shape, o_ref.dtype)


def _ffn(xs, block_expert, n_used, wg, wu, wd):
    p, d = xs.shape
    nb = p // EXPERT_BLOCK
    ff = EXPERT_FF
    return pl.pallas_call(
        _ffn_kernel,
        out_shape=jax.ShapeDtypeStruct((p, d), BF16),
        grid_spec=pltpu.PrefetchScalarGridSpec(
            num_scalar_prefetch=2,
            grid=(nb,),
            in_specs=[pl.BlockSpec((EXPERT_BLOCK, d), lambda b, be, nu: (b, 0)),
                      pl.BlockSpec((None, d, ff), lambda b, be, nu: (be[b], 0, 0)),
                      pl.BlockSpec((None, d, ff), lambda b, be, nu: (be[b], 0, 0)),
                      pl.BlockSpec((None, ff, d), lambda b, be, nu: (be[b], 0, 0))],
            out_specs=pl.BlockSpec((EXPERT_BLOCK, d), lambda b, be, nu: (b, 0)),
        ),
        compiler_params=_cparams(("arbitrary",)),
        name="moe_ffn",
    )(block_expert, n_used, xs, wg, wu, wd)


def _rope_tables4(s):
    inv = 1.0 / (ROPE_THETA ** (jnp.arange(0, MLA_ROPE, 2, dtype=F32) / MLA_ROPE))
    ang = jnp.arange(s, dtype=F32)[:, None] * inv[None, :]
    cos, sin = jnp.cos(ang), jnp.sin(ang)
    return jnp.tile(cos, (1, 8)), jnp.tile(sin, (1, 8))


def _moe_layer(x, ln_g, w_group, b_group, w_expert, b_expert, wg, wu, wd):
    s, d = x.shape
    ng, ne = N_EXPERT_GROUPS, N_EXPERTS
    w_r = jnp.zeros((d, ROUTER_LANES), F32).at[:, :ng].set(w_group).at[:, ng:ng + ne].set(w_expert)
    w_hi = w_r.astype(BF16)
    w_lo = (w_r - w_hi.astype(F32)).astype(BF16)
    b_r = jnp.zeros((1, ROUTER_LANES), F32).at[0, :ng].set(b_group).at[0, ng:ng + ne].set(b_expert)
    ht, route, cnt = _router(x, ln_g, w_hi, w_lo, b_r)
    e_id = route[:, 0:2].astype(jnp.int32)
    gate = route[:, 2:4]
    rank = route[:, 4:6].astype(jnp.int32)
    counts = cnt[0, ng:ng + ne].astype(jnp.int32)
    n_assign = 2 * s
    n_blocks = -(-(n_assign + ne * (EXPERT_BLOCK - 1)) // EXPERT_BLOCK)
    padded = (counts + EXPERT_BLOCK - 1) // EXPERT_BLOCK * EXPERT_BLOCK
    pad_ends = jnp.cumsum(padded)
    pad_starts = pad_ends - padded
    dest = pad_starts[e_id] + rank
    block_starts = jnp.arange(n_blocks, dtype=jnp.int32) * EXPERT_BLOCK
    block_expert = jnp.minimum(jnp.searchsorted(pad_ends, block_starts, side='right'), ne - 1).astype(jnp.int32)
    n_used = (pad_ends[-1] // EXPERT_BLOCK).astype(jnp.int32).reshape(1)
    tok = jnp.broadcast_to(jnp.arange(s, dtype=jnp.int32)[:, None], (s, 2))
    tok_buf = jnp.full((n_blocks * EXPERT_BLOCK,), s, jnp.int32).at[dest.reshape(-1)].set(tok.reshape(-1))
    xs = jnp.concatenate([ht, jnp.zeros((1, d), BF16)], axis=0)[tok_buf]
    out_s = _ffn(xs, block_expert, n_used, wg, wu, wd)
    y0 = out_s[dest[:, 0]].astype(F32) * gate[:, 0:1]
    y1 = out_s[dest[:, 1]].astype(F32) * gate[:, 1:2]
    return x + y0 + y1


def _perm_w_in(w_in):
    sp = np.cumsum((512, 128, 128, 512, 512, 512, 256, 64, 512))
    a_q, a_k, a_v, b_a, b_g, c_q, c_kv, c_kr, d_u = jnp.split(w_in, [int(v) for v in sp[:-1]], axis=-1)
    zero = jnp.zeros((w_in.shape[0], 64), w_in.dtype)
    return jnp.concatenate([a_q, a_k, a_v, b_a, b_g, c_q, c_kv, c_kr, zero, d_u], axis=-1)


def _perm_heads(w, parts):
    k = w.shape[0]
    wh = w.reshape(k, MLA_HEADS, sum(parts))
    offs = np.cumsum((0,) + tuple(parts))
    return jnp.concatenate([wh[:, :, offs[j]:offs[j + 1]].reshape(k, -1) for j in range(len(parts))], axis=-1)


def kernel(x, mem, rel_bias, ln_mix_g, w_in, swa_qn_g, swa_kn_g, swa_sinks, conv_dw_w, conv_dw_b, conv_ln_g, conv_ln_b, conv_pw_w, conv_pw_b, mla_qa_g, mla_kva_g, mla_wq_b, mla_wkv_b, mla_qn_nope_g, mla_qn_rope_g, mla_kn_nope_g, mla_kn_rope_g, ssm_a_re, ssm_a_im, ssm_log_dt, ssm_b_re, ssm_b_im, ssm_c_re, ssm_c_im, ssm_d, ssm_glu_w, ssm_glu_b, mix_out_g, w_out, ln_cross_g, mem_ln_g, xa_wq, xa_wk, xa_wv, xa_qn_g, xa_kn_g, xa_wo, ln_moe_g, moe_w_group, moe_b_group, moe_w_expert, moe_b_expert, moe_w_gate, moe_w_up, moe_w_down):
    b, s, d = x.shape
    assert b == 1 and d == D_MODEL and s % 1024 == 0
    xf = x.reshape(s, d)
    memf = mem.reshape(mem.shape[1], d)
    cos4, sin4 = _rope_tables4(s)
    bias_tab = _bias_table(rel_bias)
    nch = s // SSM_CHUNK
    g_n, n_in = SSM_GROUPS, SSM_GROUP_IN
    for l in range(DEPTH):
        pa, pb, pc, pd = _norm_proj(xf, ln_mix_g[l], _perm_w_in(w_in[l]).astype(BF16), P_SEGS, 256)
        o_a = _swa(pa, swa_qn_g[l], swa_kn_g[l], swa_sinks[l], bias_tab)
        o_b = _conv(pb, conv_dw_w[l], conv_dw_b[l], conv_ln_g[l], conv_ln_b[l], conv_pw_w[l].astype(BF16),
                    conv_pw_b[l])
        q, k, v = _mla_proj(pc, cos4, sin4, mla_qa_g[l], mla_kva_g[l],
                            _perm_heads(mla_wq_b[l], (MLA_NOPE, MLA_ROPE)).astype(BF16),
                            _perm_heads(mla_wkv_b[l], (MLA_NOPE, MLA_V)).astype(BF16),
                            mla_qn_nope_g[l], jnp.tile(mla_qn_rope_g[l], MLA_HEADS), mla_kn_nope_g[l],
                            jnp.concatenate([mla_kn_rope_g[l], jnp.zeros((64,), F32)]))
        o_c = _flash(q, k, v)
        u_r = pd.astype(BF16).reshape(nch, SSM_CHUNK, g_n, n_in).transpose(2, 0, 1, 3).reshape(g_n, nch, -1)
        y_r = _ssm(u_r, ssm_a_re[l], ssm_a_im[l], ssm_log_dt[l], ssm_b_re[l], ssm_b_im[l], ssm_c_re[l],
                   ssm_c_im[l], ssm_d[l])
        y_d = y_r.reshape(g_n, nch, SSM_CHUNK, n_in).transpose(1, 2, 0, 3).reshape(s, GROUP_WIDTH)
        xf = _mix_out(xf, o_a, o_b, o_c, y_d, ssm_glu_w[l].astype(BF16), ssm_glu_b[l], mix_out_g[l],
                      w_out[l].astype(BF16))
        (kv_mem,) = _norm_proj(memf, mem_ln_g[l], jnp.concatenate([xa_wk[l], xa_wv[l]], axis=-1).astype(BF16),
                               (2 * XA_HEADS * XA_HEAD_DIM,), memf.shape[0])
        xf = _xattn(xf, ln_cross_g[l], xa_wq[l].astype(BF16), xa_qn_g[l], kv_mem, xa_kn_g[l],
                    xa_wo[l].astype(BF16))
        xf = _moe_layer(xf, ln_moe_g[l], moe_w_group[l], moe_b_group[l], moe_w_expert[l], moe_b_expert[l],
                        moe_w_gate[l].astype(BF16), moe_w_up[l].astype(BF16), moe_w_down[l].astype(BF16))
    return xf.reshape(b, s, d)
```

```python
import functools
import math

import jax
import jax.numpy as jnp
import numpy as np
from jax import lax
from jax.experimental import pallas as pl
from jax.experimental.pallas import tpu as pltpu

F32 = jnp.float32
BF16 = jnp.bfloat16

D_MODEL = 2048
DEPTH = 4
GROUP_WIDTH = 512
SWA_HEADS = 8
SWA_KV_HEADS = 2
SWA_HEAD_DIM = 64
SWA_BLOCK = 128
CONV_WIDTH = 31
MLA_HEADS = 4
MLA_NOPE = 128
MLA_ROPE = 64
MLA_V = 128
MLA_QK_PAD = 256
ROPE_THETA = 10000.0
SSM_GROUPS = 32
SSM_GROUP_IN = 16
SSM_STATE = 64
SSM_CHUNK = 64
REL_BUCKETS = 32
REL_MAX_DIST = 128
XA_HEADS = 4
XA_HEAD_DIM = 128
N_EXPERT_GROUPS = 4
EXPERTS_PER_GROUP = 8
N_EXPERTS = 32
EXPERT_FF = 512
EXPERT_BLOCK = 256
NORM_EPS = 1e-6
NEG_INF = -1e30
LOG2E = 1.4426950408889634

PA_W = 768
PB_W = 1024
PC_W = 896
PD_W = 512
P_SEGS = (PA_W, PB_W, PC_W, PD_W)

V7X_VMEM_LIMIT = 48 * 1024 * 1024

NT_DIMS = (((1,), (1,)), ((), ()))


def _cparams(sem, vmem=V7X_VMEM_LIMIT):
    return pltpu.CompilerParams(dimension_semantics=sem, vmem_limit_bytes=vmem)


def _const_spec(shape):
    nd = len(shape)
    return pl.BlockSpec(shape, lambda *_: (0,) * nd, pipeline_mode=pl.Buffered(1))


def _rms(x, g):
    return x * lax.rsqrt(jnp.mean(x * x, axis=-1, keepdims=True) + NORM_EPS) * g


def _dot(a, b):
    return jnp.dot(a, b, preferred_element_type=F32)


def _dot_nt(a, b):
    return lax.dot_general(a, b, NT_DIMS, preferred_element_type=F32)


def _norm_proj_kernel(x_ref, g_ref, w_ref, *o_refs, segs):
    h = _rms(x_ref[...], g_ref[...]).astype(BF16)
    off = 0
    for o_ref, n in zip(o_refs, segs):
        o_ref[...] = _dot(h, w_ref[:, off:off + n])
        off += n


def _norm_proj(x, g, w, segs, tm):
    s, d = x.shape
    n = sum(segs)
    return pl.pallas_call(
        functools.partial(_norm_proj_kernel, segs=segs),
        out_shape=[jax.ShapeDtypeStruct((s, n_i), F32) for n_i in segs],
        grid=(s // tm,),
        in_specs=[pl.BlockSpec((tm, d), lambda i: (i, 0)), _const_spec((1, d)), _const_spec((d, n))],
        out_specs=[pl.BlockSpec((tm, n_i), lambda i: (i, 0)) for n_i in segs],
        compiler_params=_cparams(("parallel",)),
        name="norm_proj",
    )(x, g.reshape(1, d), w)


def _t5_buckets():
    qi = np.arange(SWA_BLOCK)[:, None]
    kj = np.arange(2 * SWA_BLOCK)[None, :]
    dist = SWA_BLOCK + qi - kj
    valid = (dist >= 0) & (dist < SWA_BLOCK)
    max_exact = REL_BUCKETS // 2
    d0 = np.maximum(dist, 0)
    dl = np.maximum(dist, 1).astype(np.float32)
    large = max_exact + (np.log(dl / max_exact) / np.log(REL_MAX_DIST / max_exact)
                         * (REL_BUCKETS - max_exact)).astype(np.int32)
    large = np.minimum(large, REL_BUCKETS - 1)
    bucket = np.where(d0 < max_exact, d0, large).astype(np.int32)
    return bucket, valid.astype(np.int32)


def _bias_kernel(rb_ref, bucket_ref, valid_ref, o_ref):
    b = bucket_ref[...]
    ok = valid_ref[...] > 0
    for h in range(SWA_HEADS):
        acc = jnp.zeros(b.shape, F32)
        for k in range(REL_BUCKETS):
            acc = jnp.where(b == k, rb_ref[k, h], acc)
        o_ref[h] = jnp.where(ok, acc, NEG_INF)


def _bias_table(rel_bias):
    bucket, valid = _t5_buckets()
    tab = pl.pallas_call(
        _bias_kernel,
        out_shape=jax.ShapeDtypeStruct((SWA_HEADS, SWA_BLOCK, 2 * SWA_BLOCK), F32),
        in_specs=[pl.BlockSpec(memory_space=pltpu.SMEM),
                  pl.BlockSpec(memory_space=pltpu.VMEM), pl.BlockSpec(memory_space=pltpu.VMEM)],
        out_specs=pl.BlockSpec(memory_space=pltpu.VMEM),
        name="t5_bias_table",
    )(rel_bias, jnp.asarray(bucket), jnp.asarray(valid))
    grp = SWA_HEADS // SWA_KV_HEADS
    return tab.reshape(SWA_KV_HEADS, grp * SWA_BLOCK, 2 * SWA_BLOCK)


SWA_QB = 4


def _swa_kernel(sink_ref, cur_ref, prev_ref, qg_ref, kg_ref, bias_ref, o_ref):
    i = pl.program_id(0)
    blk, dh = SWA_BLOCK, SWA_HEAD_DIM
    grp = SWA_HEADS // SWA_KV_HEADS
    qg = qg_ref[...]
    kg = kg_ref[...]
    scale = dh ** -0.5
    kv_all = jnp.concatenate([prev_ref[:, 512:768], cur_ref[:, 512:768]], axis=0)
    col = lax.broadcasted_iota(jnp.int32, (grp * blk, 2 * blk), 1)
    for kh in range(SWA_KV_HEADS):
        kn = _rms(kv_all[:, kh * dh:(kh + 1) * dh], kg).astype(BF16)
        vv = kv_all[:, 128 + kh * dh:128 + (kh + 1) * dh].astype(BF16)
        sink = jnp.concatenate([jnp.full((blk, 1), sink_ref[kh * grp + g], F32) for g in range(grp)], axis=0)
        for j in range(SWA_QB):
            rows = slice(j * blk, (j + 1) * blk)
            qs = jnp.concatenate(
                [_rms(cur_ref[rows, (kh * grp + g) * dh:(kh * grp + g + 1) * dh], qg) for g in range(grp)],
                axis=0)
            kb = kn[j * blk:(j + 2) * blk]
            vb = vv[j * blk:(j + 2) * blk]
            logits = _dot_nt((qs * scale).astype(BF16), kb) + bias_ref[kh]
            if j == 0:
                logits = jnp.where(jnp.logical_and(i == 0, col < blk), NEG_INF, logits)
            m = jnp.maximum(jnp.max(logits, axis=-1, keepdims=True), sink)
            p = jnp.exp(logits - m)
            denom = jnp.sum(p, axis=-1, keepdims=True) + jnp.exp(sink - m)
            out = _dot(p.astype(BF16), vb) / denom
            for g in range(grp):
                hq = kh * grp + g
                o_ref[rows, hq * dh:(hq + 1) * dh] = out[g * blk:(g + 1) * blk]


def _swa(pa, qn_g, kn_g, sinks, bias_tab):
    s = pa.shape[0]
    tq = SWA_QB * SWA_BLOCK
    grp = SWA_HEADS // SWA_KV_HEADS
    return pl.pallas_call(
        _swa_kernel,
        out_shape=jax.ShapeDtypeStruct((s, GROUP_WIDTH), F32),
        grid_spec=pltpu.PrefetchScalarGridSpec(
            num_scalar_prefetch=1,
            grid=(s // tq,),
            in_specs=[pl.BlockSpec((tq, PA_W), lambda i, sk: (i, 0)),
                      pl.BlockSpec((SWA_BLOCK, PA_W), lambda i, sk: (jnp.maximum(i * SWA_QB - 1, 0), 0)),
                      _const_spec((1, SWA_HEAD_DIM)), _const_spec((1, SWA_HEAD_DIM)),
                      _const_spec((SWA_KV_HEADS, grp * SWA_BLOCK, 2 * SWA_BLOCK))],
            out_specs=pl.BlockSpec((tq, GROUP_WIDTH), lambda i, sk: (i, 0)),
        ),
        compiler_params=_cparams(("parallel",)),
        name="swa_attention",
    )(sinks, pa, pa, qn_g.reshape(1, -1), kn_g.reshape(1, -1), bias_tab)


CONV_TILE = 256
CONV_HALO = 32


def _conv_kernel(cur_ref, prev_ref, dww_ref, dwb_ref, lng_ref, lnb_ref, pww_ref, pwb_ref, o_ref, z_scr):
    i = pl.program_id(0)
    t, c = CONV_TILE, GROUP_WIDTH
    z_scr[CONV_HALO:CONV_HALO + t, :] = cur_ref[:, 0:c] * jax.nn.sigmoid(cur_ref[:, c:2 * c])
    zp = prev_ref[:, 0:c] * jax.nn.sigmoid(prev_ref[:, c:2 * c])
    z_scr[0:CONV_HALO, :] = jnp.where(i > 0, zp, 0.0)
    base = CONV_HALO - (CONV_WIDTH - 1)
    acc = jnp.zeros((t, c), F32)
    for k in range(CONV_WIDTH):
        acc = acc + dww_ref[k:k + 1, :] * z_scr[base + k:base + k + t, :]
    acc = acc + dwb_ref[...]
    xc = acc - jnp.mean(acc, axis=-1, keepdims=True)
    y = xc * lax.rsqrt(jnp.mean(xc * xc, axis=-1, keepdims=True) + NORM_EPS) * lng_ref[...] + lnb_ref[...]
    y = y * jax.nn.sigmoid(y)
    o_ref[...] = _dot(y.astype(BF16), pww_ref[...]) + pwb_ref[...]


def _conv(pb, dw_w, dw_b, ln_g, ln_b, pw_w, pw_b):
    s = pb.shape[0]
    c = GROUP_WIDTH
    r = CONV_TILE // CONV_HALO
    row = lambda v: v.reshape(1, c)
    return pl.pallas_call(
        _conv_kernel,
        out_shape=jax.ShapeDtypeStruct((s, c), F32),
        grid=(s // CONV_TILE,),
        in_specs=[pl.BlockSpec((CONV_TILE, PB_W), lambda i: (i, 0)),
                  pl.BlockSpec((CONV_HALO, PB_W), lambda i: (jnp.maximum(i * r - 1, 0), 0)),
                  _const_spec((CONV_WIDTH, c)), _const_spec((1, c)), _const_spec((1, c)), _const_spec((1, c)),
                  _const_spec((c, c)), _const_spec((1, c))],
        out_specs=pl.BlockSpec((CONV_TILE, c), lambda i: (i, 0)),
        scratch_shapes=[pltpu.VMEM((CONV_TILE + CONV_HALO, c), F32)],
        compiler_params=_cparams(("parallel",)),
        name="conformer_conv",
    )(pb, pb, dw_w, row(dw_b), row(ln_g), row(ln_b), pw_w, row(pw_b))


MLA_TILE = 256


def _rot_half64(x):
    n = x.shape[-1]
    ax = x.ndim - 1
    lane = lax.broadcasted_iota(jnp.int32, x.shape, ax)
    return jnp.where((lane % 64) < 32, -pltpu.roll(x, n - 32, ax), pltpu.roll(x, 32, ax))


def _seg_rms64(x):
    n = x.shape[-1]
    lane = lax.broadcasted_iota(jnp.int32, x.shape, x.ndim - 1)
    x2 = x * x
    r = jnp.zeros(x.shape, F32)
    for h in range(n // 64):
        msk = jnp.logical_and(lane >= h * 64, lane < (h + 1) * 64)
        sm = jnp.sum(jnp.where(msk, x2, 0.0), axis=-1, keepdims=True)
        r = jnp.where(msk, lax.rsqrt(sm * (1.0 / 64) + NORM_EPS), r)
    return x * r


def _mla_proj_kernel(pc_ref, cos_ref, sin_ref, qag_ref, kvag_ref, wq_ref, wkv_ref,
                     qng_ref, qrg_ref, kng_ref, krg_ref, q_ref, k_ref, v_ref):
    h_n, dn = MLA_HEADS, MLA_NOPE
    scale = (MLA_NOPE + MLA_ROPE) ** -0.5 * LOG2E
    cos4 = cos_ref[...]
    sin4 = sin_ref[...]
    qn = _rms(pc_ref[:, 0:512], qag_ref[...]).astype(BF16)
    q = _dot(qn, wq_ref[...])
    kvn = _rms(pc_ref[:, 512:768], kvag_ref[...]).astype(BF16)
    kv = _dot(kvn, wkv_ref[...])
    qr = _seg_rms64(q[:, h_n * dn:]) * qrg_ref[...]
    qr = (qr * cos4 + _rot_half64(qr) * sin4) * scale
    kr = pc_ref[:, 768:896]
    kr = kr * lax.rsqrt(jnp.sum(kr * kr, axis=-1, keepdims=True) * (1.0 / MLA_ROPE) + NORM_EPS) * krg_ref[...]
    kr = (kr * cos4[:, 0:128] + _rot_half64(kr) * sin4[:, 0:128]).astype(BF16)
    lane = lax.broadcasted_iota(jnp.int32, (q.shape[0], 128), 1)
    for h in range(h_n):
        qnope = _rms(q[:, h * dn:(h + 1) * dn], qng_ref[...]) * scale
        blk = qr[:, (h // 2) * 128:(h // 2 + 1) * 128]
        if h % 2 == 1:
            blk = pltpu.roll(blk, 64, 1)
        blk = jnp.where(lane < 64, blk, 0.0)
        q_ref[h] = jnp.concatenate([qnope, blk], axis=-1).astype(BF16)
        knope = _rms(kv[:, h * dn:(h + 1) * dn], kng_ref[...]).astype(BF16)
        k_ref[h] = jnp.concatenate([knope, kr], axis=-1)
        v_ref[h] = kv[:, (h_n + h) * dn:(h_n + h + 1) * dn].astype(BF16)


def _mla_proj(pc, cos4, sin4, qa_g, kva_g, wq, wkv, qn_nope_g, qn_rope_g4, kn_nope_g, kn_rope_g_pad):
    s = pc.shape[0]
    t = MLA_TILE
    h = MLA_HEADS
    row = lambda v: v.reshape(1, -1)
    return pl.pallas_call(
        _mla_proj_kernel,
        out_shape=[jax.ShapeDtypeStruct((h, s, MLA_QK_PAD), BF16),
                   jax.ShapeDtypeStruct((h, s, MLA_QK_PAD), BF16),
                   jax.ShapeDtypeStruct((h, s, MLA_V), BF16)],
        grid=(s // t,),
        in_specs=[pl.BlockSpec((t, PC_W), lambda i: (i, 0)),
                  pl.BlockSpec((t, 256), lambda i: (i, 0)), pl.BlockSpec((t, 256), lambda i: (i, 0)),
                  _const_spec((1, 512)), _const_spec((1, 256)),
                  _const_spec((512, 768)), _const_spec((256, 1024)),
                  _const_spec((1, 128)), _const_spec((1, 256)), _const_spec((1, 128)), _const_spec((1, 128))],
        out_specs=[pl.BlockSpec((h, t, MLA_QK_PAD), lambda i: (0, i, 0)),
                   pl.BlockSpec((h, t, MLA_QK_PAD), lambda i: (0, i, 0)),
                   pl.BlockSpec((h, t, MLA_V), lambda i: (0, i, 0))],
        compiler_params=_cparams(("parallel",)),
        name="mla_proj",
    )(pc, cos4, sin4, row(qa_g), row(kva_g), wq, wkv, row(qn_nope_g), row(qn_rope_g4), row(kn_nope_g),
      row(kn_rope_g_pad))


FLASH_BQ = 1024
FLASH_BK = 512


def _flash_kernel(q_ref, k_ref, vt_ref, o_ref, sa_scr, sb_scr, m_scr, l_scr, acc_scr):
    bq, bk = FLASH_BQ, FLASH_BK
    assert bq == 2 * bk
    qi = pl.program_id(1)
    m_scr[...] = jnp.full(m_scr.shape, NEG_INF, F32)
    l_scr[...] = jnp.zeros(l_scr.shape, F32)
    acc_scr[...] = jnp.zeros(acc_scr.shape, F32)

    def scores(j, s_scr):
        k = k_ref[pl.ds(pl.multiple_of(j * bk, bk), bk), :]
        s_scr[...] = _dot_nt(k, q_ref[...])

    def update(j, s_scr, masked):
        st = s_scr[...]
        if masked:
            kpos = j * bk + lax.broadcasted_iota(jnp.int32, (bk, bq), 0)
            qpos = qi * bq + lax.broadcasted_iota(jnp.int32, (bk, bq), 1)
            st = jnp.where(kpos <= qpos, st, NEG_INF)
        m_prev = m_scr[...]
        m_new = jnp.maximum(m_prev, jnp.max(st, axis=0, keepdims=True))
        alpha = jnp.exp2(m_prev - m_new)
        p = jnp.exp2(st - m_new)
        l_scr[...] = alpha * l_scr[...] + jnp.sum(p, axis=0, keepdims=True)
        acc_scr[...] = alpha * acc_scr[...] + _dot(vt_ref[j], p.astype(BF16))
        m_scr[...] = m_new

    scores(0, sa_scr)

    def body(i, carry):
        scores(2 * i + 1, sb_scr)
        update(2 * i, sa_scr, False)
        scores(2 * i + 2, sa_scr)
        update(2 * i + 1, sb_scr, False)
        return carry

    lax.fori_loop(0, qi, body, 0)
    scores(2 * qi + 1, sb_scr)
    update(2 * qi, sa_scr, True)
    update(2 * qi + 1, sb_scr, True)
    o_ref[...] = (acc_scr[...] / l_scr[...]).T


def _flash(q, k, vt):
    h, s, _ = q.shape
    nkb = s // FLASH_BK
    return pl.pallas_call(
        _flash_kernel,
        out_shape=jax.ShapeDtypeStruct((s, h * MLA_V), F32),
        grid=(h, s // FLASH_BQ),
        in_specs=[pl.BlockSpec((None, FLASH_BQ, MLA_QK_PAD), lambda hh, i: (hh, i, 0)),
                  pl.BlockSpec((None, s, MLA_QK_PAD), lambda hh, i: (hh, 0, 0)),
                  pl.BlockSpec((None, nkb, MLA_V, FLASH_BK), lambda hh, i: (hh, 0, 0, 0))],
        out_specs=pl.BlockSpec((FLASH_BQ, MLA_V), lambda hh, i: (i, hh)),
        scratch_shapes=[pltpu.VMEM((FLASH_BK, FLASH_BQ), F32), pltpu.VMEM((FLASH_BK, FLASH_BQ), F32),
                        pltpu.VMEM((1, FLASH_BQ), F32), pltpu.VMEM((1, FLASH_BQ), F32),
                        pltpu.VMEM((MLA_V, FLASH_BQ), F32)],
        compiler_params=_cparams(("parallel", "arbitrary")),
        name="mla_flash",
    )(q, k, vt)


def _split3_dot(a, sel):
    a1 = a.astype(BF16)
    r1 = a - a1.astype(F32)
    a2 = r1.astype(BF16)
    a3 = (r1 - a2.astype(F32)).astype(BF16)
    return _dot(a1, sel) + _dot(a2, sel) + _dot(a3, sel)


def _hi_dot(a, b):
    return lax.dot_general(a, b, (((1,), (0,)), ((), ())), precision=lax.Precision.HIGHEST,
                           preferred_element_type=F32)


def _ssm_kernel(u_ref, arc_ref, aic_ref, arr_ref, air_ref, ldt_ref, bre_ref, bim_ref, btre_ref, btim_ref,
                ctre_ref, ctim_ref, dcol_ref, y_ref, toep_scr, *, nch):
    tc, n_in, n_st = SSM_CHUNK, SSM_GROUP_IN, SSM_STATE
    w = tc * n_in
    dt = jnp.exp(ldt_ref[...])
    ar_c, ai_c = arc_ref[...], aic_ref[...]
    ar_r, ai_r = arr_ref[...], air_ref[...]

    def lam_pow(ar, ai, e):
        mag = jnp.exp(ar * dt * e)
        ang = ai * dt * e
        return mag * jnp.cos(ang), mag * jnp.sin(ang)

    def zoh_coef(ar, ai):
        lr, li = lam_pow(ar, ai, 1.0)
        x, y = lr - 1.0, li
        den = ar * ar + ai * ai
        return (x * ar + y * ai) / den, (y * ar - x * ai) / den

    lane_w = lax.broadcasted_iota(jnp.int32, (tc, w), 1)
    sel_tau = (lane_w // n_in == lax.broadcasted_iota(jnp.int32, (tc, w), 0)).astype(BF16)
    lane_o = lax.broadcasted_iota(jnp.int32, (n_in, w), 1)
    sel_o = (lane_o % n_in == lax.broadcasted_iota(jnp.int32, (n_in, w), 0)).astype(BF16)

    tau = lax.broadcasted_iota(jnp.int32, (n_st, tc), 1).astype(F32)
    lre, lim = lam_pow(ar_c, ai_c, tau)
    lre_x, lim_x = _split3_dot(lre, sel_tau), _split3_dot(lim, sel_tau)
    rre, rim = lam_pow(ar_c, ai_c, (tc - 1.0) - tau)
    rre_x, rim_x = _split3_dot(rre, sel_tau), _split3_dot(rim, sel_tau)
    ctre_x, ctim_x = _split3_dot(ctre_ref[...], sel_o), _split3_dot(ctim_ref[...], sel_o)

    e_re = lre_x * ctre_x - lim_x * ctim_x
    e_im = lre_x * ctim_x + lim_x * ctre_x

    cr_c, ci_c = zoh_coef(ar_c, ai_c)
    bb_re = cr_c * bre_ref[...] - ci_c * bim_ref[...]
    bb_im = cr_c * bim_ref[...] + ci_c * bre_ref[...]
    bb_re_x, bb_im_x = _split3_dot(bb_re, sel_o), _split3_dot(bb_im, sel_o)
    f_re = (rre_x * bb_re_x - rim_x * bb_im_x).astype(BF16)
    f_im = (rre_x * bb_im_x + rim_x * bb_re_x).astype(BF16)

    cr_r, ci_r = zoh_coef(ar_r, ai_r)
    bbt_re = cr_r * btre_ref[...] - ci_r * btim_ref[...]
    bbt_im = cr_r * btim_ref[...] + ci_r * btre_ref[...]
    g_row = _hi_dot(bbt_re, e_re) - _hi_dot(bbt_im, e_im)
    lane_g = lax.broadcasted_iota(jnp.int32, (n_in, w), 1)
    row_g = lax.broadcasted_iota(jnp.int32, (n_in, w), 0)
    g_row = g_row + jnp.where(lane_g == row_g, dcol_ref[...], 0.0)
    for s in range(tc):
        rolled = g_row if s == 0 else pltpu.roll(g_row, n_in * s, 1)
        toep_scr[s * n_in:(s + 1) * n_in, :] = jnp.where(lane_g >= n_in * s, rolled, 0.0).astype(BF16)

    u = u_ref[...]
    s_re = _dot_nt(u, f_re)
    s_im = _dot_nt(u, f_im)
    row = lax.broadcasted_iota(jnp.int32, (nch, n_st), 0)
    sh = 1
    while sh < nch:
        pr, pi = lam_pow(ar_r, ai_r, float(tc * sh))
        re_s = jnp.where(row >= sh, pltpu.roll(s_re, sh, 0), 0.0)
        im_s = jnp.where(row >= sh, pltpu.roll(s_im, sh, 0), 0.0)
        s_re, s_im = s_re + pr * re_s - pi * im_s, s_im + pr * im_s + pi * re_s
        sh *= 2
    p_re = jnp.where(row >= 1, pltpu.roll(s_re, 1, 0), 0.0)
    p_im = jnp.where(row >= 1, pltpu.roll(s_im, 1, 0), 0.0)
    l1r, l1i = lam_pow(ar_c, ai_c, 1.0)
    h_re = (l1r * e_re - l1i * e_im).astype(BF16)
    h_im = (l1r * e_im + l1i * e_re).astype(BF16)
    y = _dot(u, toep_scr[...]) + _dot(p_re.astype(BF16), h_re) - _dot(p_im.astype(BF16), h_im)
    y_ref[...] = y


def _ssm(u_r, a_re, a_im, log_dt, b_re, b_im, c_re, c_im, d_skip):
    g, nch, w = u_r.shape
    n_st, n_in = SSM_STATE, SSM_GROUP_IN
    col = lambda v: v.reshape(g, n_st, 1)
    rw = lambda v: v.reshape(g, 1, n_st)
    spec = lambda *shape: pl.BlockSpec((None,) + shape, lambda i: (i,) + (0,) * len(shape))
    return pl.pallas_call(
        functools.partial(_ssm_kernel, nch=nch),
        out_shape=jax.ShapeDtypeStruct((g, nch, w), F32),
        grid=(g,),
        in_specs=[spec(nch, w), spec(n_st, 1), spec(n_st, 1), spec(1, n_st), spec(1, n_st), spec(1, 1),
                  spec(n_st, n_in), spec(n_st, n_in), spec(n_in, n_st), spec(n_in, n_st),
                  spec(n_st, n_in), spec(n_st, n_in), spec(n_in, 1)],
        out_specs=spec(nch, w),
        scratch_shapes=[pltpu.VMEM((w, w), BF16)],
        compiler_params=_cparams(("parallel",)),
        name="s5_ssm",
    )(u_r, col(a_re), col(a_im), rw(a_re), rw(a_im), log_dt.reshape(g, 1, 1), b_re, b_im,
      jnp.swapaxes(b_re, 1, 2), jnp.swapaxes(b_im, 1, 2), jnp.swapaxes(c_re, 1, 2), jnp.swapaxes(c_im, 1, 2),
      d_skip.reshape(g, n_in, 1))


OUT_TILE = 256


def _mix_out_kernel(x_ref, oa_ref, ob_ref, oc_ref, yd_ref, gw_ref, gb_ref, mg_ref, wo_ref, o_ref):
    c = GROUP_WIDTH
    y = jax.nn.gelu(yd_ref[...])
    od = y * jax.nn.sigmoid(_dot(y.astype(BF16), gw_ref[...]) + gb_ref[...])
    acc = x_ref[...]
    for m, o in enumerate((oa_ref[...], ob_ref[...], oc_ref[...], od)):
        n = _rms(o, mg_ref[:, m * c:(m + 1) * c]).astype(BF16)
        acc = acc + _dot(n, wo_ref[m * c:(m + 1) * c, :])
    o_ref[...] = acc


def _mix_out(x, oa, ob, oc, yd, glu_w, glu_b, mix_g, w_out):
    s, d = x.shape
    c = GROUP_WIDTH
    t = OUT_TILE
    tile = lambda n: pl.BlockSpec((t, n), lambda i: (i, 0))
    return pl.pallas_call(
        _mix_out_kernel,
        out_shape=jax.ShapeDtypeStruct((s, d), F32),
        grid=(s // t,),
        in_specs=[tile(d), tile(c), tile(c), tile(c), tile(c),
                  _const_spec((c, c)), _const_spec((1, c)), _const_spec((1, d)), _const_spec((d, d))],
        out_specs=tile(d),
        compiler_params=_cparams(("parallel",)),
        name="mix_out",
    )(x, oa, ob, oc, yd, glu_w, glu_b.reshape(1, c), mix_g.reshape(1, d), w_out)


XA_TILE = 256


def _xattn_kernel(x_ref, lg_ref, wq_ref, qg_ref, kv_ref, kg_ref, wo_ref, o_ref):
    dh = XA_HEAD_DIM
    hw = XA_HEADS * dh
    x = x_ref[...]
    q = _dot(_rms(x, lg_ref[...]).astype(BF16), wq_ref[...])
    scale = dh ** -0.5
    acc = x
    for h in range(XA_HEADS):
        qh = (_rms(q[:, h * dh:(h + 1) * dh], qg_ref[...]) * scale).astype(BF16)
        kh = _rms(kv_ref[:, h * dh:(h + 1) * dh], kg_ref[...]).astype(BF16)
        vh = kv_ref[:, hw + h * dh:hw + (h + 1) * dh].astype(BF16)
        logits = _dot_nt(qh, kh)
        p = jnp.exp(logits - jnp.max(logits, axis=-1, keepdims=True))
        oh = _dot(p.astype(BF16), vh) / jnp.sum(p, axis=-1, keepdims=True)
        acc = acc + _dot(oh.astype(BF16), wo_ref[h * dh:(h + 1) * dh, :])
    o_ref[...] = acc


def _xattn(x, ln_g, wq, qn_g, kv, kn_g, wo):
    s, d = x.shape
    m, kvw = kv.shape
    hw = XA_HEADS * XA_HEAD_DIM
    t = XA_TILE
    return pl.pallas_call(
        _xattn_kernel,
        out_shape=jax.ShapeDtypeStruct((s, d), F32),
        grid=(s // t,),
        in_specs=[pl.BlockSpec((t, d), lambda i: (i, 0)), _const_spec((1, d)), _const_spec((d, hw)),
                  _const_spec((1, XA_HEAD_DIM)), _const_spec((m, kvw)), _const_spec((1, XA_HEAD_DIM)),
                  _const_spec((hw, d))],
        out_specs=pl.BlockSpec((t, d), lambda i: (i, 0)),
        compiler_params=_cparams(("parallel",)),
        name="mem_xattn",
    )(x, ln_g.reshape(1, d), wq, qn_g.reshape(1, -1), kv, kn_g.reshape(1, -1), wo)


ROUTER_TILE = 256
ROUTER_LANES = 128


def _router_kernel(x_ref, lg_ref, whi_ref, wlo_ref, b_ref, h_ref, r_ref, cnt_ref, carry_scr):
    i = pl.program_id(0)
    t = ROUTER_TILE
    ng, epg = N_EXPERT_GROUPS, EXPERTS_PER_GROUP

    @pl.when(i == 0)
    def _():
        carry_scr[...] = jnp.zeros(carry_scr.shape, F32)

    h = _rms(x_ref[...], lg_ref[...])
    h_hi = h.astype(BF16)
    h_lo = (h - h_hi.astype(F32)).astype(BF16)
    h_ref[...] = _pack_bf16_pair(h[:, 0:D_MODEL // 2], h[:, D_MODEL // 2:])
    logits = _dot(h_hi, whi_ref[...]) + _dot(h_hi, wlo_ref[...]) + _dot(h_lo, whi_ref[...]) + b_ref[...]
    lane = lax.broadcasted_iota(jnp.int32, (t, ROUTER_LANES), 1)
    big = jnp.int32(ROUTER_LANES)

    def first_max(vals, mask):
        v = jnp.where(mask, vals, -jnp.inf)
        mx = jnp.max(v, axis=-1, keepdims=True)
        idx = jnp.min(jnp.where(jnp.logical_and(mask, v == mx), lane, big), axis=-1, keepdims=True)
        return mx, idx

    gmask = lane < ng
    gmax, gidx = first_max(logits, gmask)
    g_w = 1.0 / jnp.sum(jnp.where(gmask, jnp.exp(logits - gmax), 0.0), axis=-1, keepdims=True)
    emask = jnp.logical_and(lane >= ng + gidx * epg, lane < ng + (gidx + 1) * epg)
    v1, i1 = first_max(logits, emask)
    v2, i2 = first_max(logits, jnp.logical_and(emask, lane != i1))
    e2 = jnp.exp(v2 - v1)
    gate1 = g_w / (1.0 + e2)
    gate2 = g_w * e2 / (1.0 + e2)

    oh1 = lane == i1
    oh2 = lane == i2
    both = (oh1.astype(F32) + oh2.astype(F32))
    tri = (lax.broadcasted_iota(jnp.int32, (t, t), 1) < lax.broadcasted_iota(jnp.int32, (t, t), 0)).astype(BF16)
    before = _dot(tri, both.astype(BF16)) + carry_scr[...]
    rank1 = jnp.sum(jnp.where(oh1, before, 0.0), axis=-1, keepdims=True)
    rank2 = jnp.sum(jnp.where(oh2, before, 0.0), axis=-1, keepdims=True)
    carry_scr[...] = carry_scr[...] + jnp.sum(both, axis=0, keepdims=True)
    cnt_ref[...] = jnp.broadcast_to(carry_scr[...], cnt_ref.shape)

    out = jnp.zeros((t, ROUTER_LANES), F32)
    for k, val in enumerate(((i1 - ng).astype(F32), (i2 - ng).astype(F32), gate1, gate2, rank1, rank2)):
        out = jnp.where(lane == k, val, out)
    r_ref[...] = out


def _router(x, ln_g, w_hi, w_lo, bias):
    s, d = x.shape
    t = ROUTER_TILE
    return pl.pallas_call(
        _router_kernel,
        out_shape=[jax.ShapeDtypeStruct((s, d // 2), jnp.uint32), jax.ShapeDtypeStruct((s, ROUTER_LANES), F32),
                   jax.ShapeDtypeStruct((8, ROUTER_LANES), F32)],
        grid=(s // t,),
        in_specs=[pl.BlockSpec((t, d), lambda i: (i, 0)), _const_spec((1, d)),
                  _const_spec((d, ROUTER_LANES)), _const_spec((d, ROUTER_LANES)), _const_spec((1, ROUTER_LANES))],
        out_specs=[pl.BlockSpec((t, d // 2), lambda i: (i, 0)), pl.BlockSpec((t, ROUTER_LANES), lambda i: (i, 0)),
                   pl.BlockSpec((8, ROUTER_LANES), lambda i: (0, 0))],
        scratch_shapes=[pltpu.VMEM((1, ROUTER_LANES), F32)],
        compiler_params=_cparams(("arbitrary",)),
        name="moe_router",
    )(x, ln_g.reshape(1, d), w_hi, w_lo, bias)


def _ffn_kernel(be_ref, nu_ref, x_ref, wg_ref, wu_ref, wd_ref, o_ref, wg_scr, wu_scr, wd_scr):
    b = pl.program_id(0)
    half = D_MODEL // 2
    prev = be_ref[jnp.maximum(b - 1, 0)]

    @pl.when(jnp.logical_or(b == 0, be_ref[b] != prev))
    def _():
        wg_scr[...] = wg_ref[...].astype(BF16)
        wu_scr[...] = wu_ref[...].astype(BF16)
        wd_scr[...] = wd_ref[...].astype(BF16)

    @pl.when(b < nu_ref[0])
    def _():
        xa, xb = _unpack_bf16_pair(x_ref[...])
        xa, xb = xa.astype(BF16), xb.astype(BF16)
        g = _dot(xa, wg_scr[0:half, :]) + _dot(xb, wg_scr[half:, :])
        u = _dot(xa, wu_scr[0:half, :]) + _dot(xb, wu_scr[half:, :])
        hid = ((g * jax.nn.sigmoid(g)) * u).astype(BF16)
        o_ref[...] = _pack_bf16_pair(_dot(hid, wd_scr[:, 0:half]), _dot(hid, wd_scr[:, half:]))

    @pl.when(b >= nu_ref[0])
    def _():
        o_ref[...] = jnp.zeros(o_ref.shape, o_ref.dtype)


def _ffn(xs, block_expert, n_used, wg, wu, wd):
    p, half = xs.shape
    d = 2 * half
    nb = p // EXPERT_BLOCK
    ff = EXPERT_FF
    return pl.pallas_call(
        _ffn_kernel,
        out_shape=jax.ShapeDtypeStruct((p, half), jnp.uint32),
        grid_spec=pltpu.PrefetchScalarGridSpec(
            num_scalar_prefetch=2,
            grid=(nb,),
            in_specs=[pl.BlockSpec((EXPERT_BLOCK, half), lambda b, be, nu: (b, 0)),
                      pl.BlockSpec((None, d, ff), lambda b, be, nu: (be[b], 0, 0)),
                      pl.BlockSpec((None, d, ff), lambda b, be, nu: (be[b], 0, 0)),
                      pl.BlockSpec((None, ff, d), lambda b, be, nu: (be[b], 0, 0))],
            out_specs=pl.BlockSpec((EXPERT_BLOCK, half), lambda b, be, nu: (b, 0)),
            scratch_shapes=[pltpu.VMEM((d, ff), BF16), pltpu.VMEM((d, ff), BF16), pltpu.VMEM((ff, d), BF16)],
        ),
        compiler_params=_cparams(("arbitrary",), vmem=56 * 1024 * 1024),
        name="moe_ffn",
    )(block_expert, n_used, xs, wg, wu, wd)


MOE_ROW_TILE = 256


def _pack_bf16_pair(a, b):
    hi = lax.bitcast_convert_type(a.astype(BF16).astype(F32), jnp.uint32)
    lo = lax.bitcast_convert_type(b.astype(BF16).astype(F32), jnp.uint32)
    return hi | (lo >> 16)


def _unpack_bf16_pair(w):
    a = lax.bitcast_convert_type(w & jnp.uint32(0xFFFF0000), F32)
    b = lax.bitcast_convert_type(w << 16, F32)
    return a, b


def _dispatch_kernel(dest_ref, h_hbm, init_hbm, xs_hbm, sem):
    del init_hbm
    t = MOE_ROW_TILE
    base = pl.program_id(0) * t

    def row_copy(r, k):
        return pltpu.make_async_copy(h_hbm.at[pl.ds(base + r, 1)], xs_hbm.at[pl.ds(dest_ref[0, 0, 2 * r + k], 1)], sem)

    def issue(r, carry):
        row_copy(r, 0).start()
        row_copy(r, 1).start()
        return carry

    def drain(r, carry):
        row_copy(r, 0).wait()
        row_copy(r, 1).wait()
        return carry

    lax.fori_loop(0, t, issue, 0, unroll=8)
    lax.fori_loop(0, t, drain, 0, unroll=8)


def _dispatch(h_packed, dest, n_rows):
    s, half = h_packed.shape
    t = MOE_ROW_TILE
    return pl.pallas_call(
        _dispatch_kernel,
        out_shape=jax.ShapeDtypeStruct((n_rows, half), jnp.uint32),
        grid=(s // t,),
        in_specs=[pl.BlockSpec((1, 1, 2 * t), lambda i: (i, 0, 0), memory_space=pltpu.SMEM),
                  pl.BlockSpec(memory_space=pl.ANY), pl.BlockSpec(memory_space=pl.ANY)],
        out_specs=pl.BlockSpec(memory_space=pl.ANY),
        scratch_shapes=[pltpu.SemaphoreType.DMA(())],
        input_output_aliases={2: 0},
        compiler_params=_cparams(("arbitrary",)),
        name="moe_dispatch",
    )(dest.reshape(s // t, 1, 2 * t), h_packed, jnp.zeros((n_rows, half), jnp.uint32))


def _combine_kernel(dest_ref, x_ref, route_ref, rows_hbm, o_ref, buf, sem):
    t = MOE_ROW_TILE
    half = D_MODEL // 2

    def row_copy(r, k):
        return pltpu.make_async_copy(rows_hbm.at[pl.ds(dest_ref[0, 0, 2 * r + k], 1)], buf.at[pl.ds(k * t + r, 1)], sem)

    def issue(r, carry):
        row_copy(r, 0).start()
        row_copy(r, 1).start()
        return carry

    def drain(r, carry):
        row_copy(r, 0).wait()
        row_copy(r, 1).wait()
        return carry

    lax.fori_loop(0, t, issue, 0, unroll=8)
    lax.fori_loop(0, t, drain, 0, unroll=8)
    g0 = route_ref[:, 2:3]
    g1 = route_ref[:, 3:4]
    a0, b0 = _unpack_bf16_pair(buf[0:t, :])
    a1, b1 = _unpack_bf16_pair(buf[t:2 * t, :])
    o_ref[:, 0:half] = x_ref[:, 0:half] + g0 * a0 + g1 * a1
    o_ref[:, half:] = x_ref[:, half:] + g0 * b0 + g1 * b1


def _combine(x, route, rows, dest):
    s, d = x.shape
    t = MOE_ROW_TILE
    return pl.pallas_call(
        _combine_kernel,
        out_shape=jax.ShapeDtypeStruct((s, d), F32),
        grid=(s // t,),
        in_specs=[pl.BlockSpec((1, 1, 2 * t), lambda i: (i, 0, 0), memory_space=pltpu.SMEM),
                  pl.BlockSpec((t, d), lambda i: (i, 0)), pl.BlockSpec((t, ROUTER_LANES), lambda i: (i, 0)),
                  pl.BlockSpec(memory_space=pl.ANY)],
        out_specs=pl.BlockSpec((t, d), lambda i: (i, 0)),
        scratch_shapes=[pltpu.VMEM((2 * t, d // 2), jnp.uint32), pltpu.SemaphoreType.DMA(())],
        compiler_params=_cparams(("arbitrary",)),
        name="moe_combine",
    )(dest.reshape(s // t, 1, 2 * t), x, route, rows)


def _rope_tables4(s):
    inv = 1.0 / (ROPE_THETA ** (jnp.arange(0, MLA_ROPE, 2, dtype=F32) / MLA_ROPE))
    ang = jnp.arange(s, dtype=F32)[:, None] * inv[None, :]
    cos, sin = jnp.cos(ang), jnp.sin(ang)
    return jnp.tile(cos, (1, 8)), jnp.tile(sin, (1, 8))


def _moe_layer(x, ln_g, w_group, b_group, w_expert, b_expert, wg, wu, wd):
    s, d = x.shape
    ng, ne = N_EXPERT_GROUPS, N_EXPERTS
    w_r = jnp.zeros((d, ROUTER_LANES), F32).at[:, :ng].set(w_group).at[:, ng:ng + ne].set(w_expert)
    w_hi = w_r.astype(BF16)
    w_lo = (w_r - w_hi.astype(F32)).astype(BF16)
    b_r = jnp.zeros((1, ROUTER_LANES), F32).at[0, :ng].set(b_group).at[0, ng:ng + ne].set(b_expert)
    h_packed, route, cnt = _router(x, ln_g, w_hi, w_lo, b_r)
    e_id = route[:, 0:2].astype(jnp.int32)
    rank = route[:, 4:6].astype(jnp.int32)
    counts = cnt[0, ng:ng + ne].astype(jnp.int32)
    n_blocks = -(-(2 * s + ne * (EXPERT_BLOCK - 1)) // EXPERT_BLOCK)
    padded = (counts + EXPERT_BLOCK - 1) // EXPERT_BLOCK * EXPERT_BLOCK
    pad_ends = jnp.cumsum(padded)
    pad_starts = pad_ends - padded
    dest = pad_starts[e_id] + rank
    block_starts = jnp.arange(n_blocks, dtype=jnp.int32) * EXPERT_BLOCK
    block_expert = jnp.minimum(jnp.sum(pad_ends[None, :] <= block_starts[:, None], axis=1), ne - 1).astype(jnp.int32)
    n_used = (pad_ends[-1] // EXPERT_BLOCK).astype(jnp.int32).reshape(1)
    xs = _dispatch(h_packed, dest, n_blocks * EXPERT_BLOCK)
    rows = _ffn(xs, block_expert, n_used, wg, wu, wd)
    return _combine(x, route, rows, dest)


def _perm_w_in(w_in):
    sp = np.cumsum((512, 128, 128, 512, 512, 512, 256, 64, 512))
    a_q, a_k, a_v, b_a, b_g, c_q, c_kv, c_kr, d_u = jnp.split(w_in, [int(v) for v in sp[:-1]], axis=-1)
    zero = jnp.zeros((w_in.shape[0], 64), w_in.dtype)
    return jnp.concatenate([a_q, a_k, a_v, b_a, b_g, c_q, c_kv, c_kr, zero, d_u], axis=-1)


def _perm_heads(w, parts):
    k = w.shape[0]
    wh = w.reshape(k, MLA_HEADS, sum(parts))
    offs = np.cumsum((0,) + tuple(parts))
    return jnp.concatenate([wh[:, :, offs[j]:offs[j + 1]].reshape(k, -1) for j in range(len(parts))], axis=-1)


def kernel(x, mem, rel_bias, ln_mix_g, w_in, swa_qn_g, swa_kn_g, swa_sinks, conv_dw_w, conv_dw_b, conv_ln_g, conv_ln_b, conv_pw_w, conv_pw_b, mla_qa_g, mla_kva_g, mla_wq_b, mla_wkv_b, mla_qn_nope_g, mla_qn_rope_g, mla_kn_nope_g, mla_kn_rope_g, ssm_a_re, ssm_a_im, ssm_log_dt, ssm_b_re, ssm_b_im, ssm_c_re, ssm_c_im, ssm_d, ssm_glu_w, ssm_glu_b, mix_out_g, w_out, ln_cross_g, mem_ln_g, xa_wq, xa_wk, xa_wv, xa_qn_g, xa_kn_g, xa_wo, ln_moe_g, moe_w_group, moe_b_group, moe_w_expert, moe_b_expert, moe_w_gate, moe_w_up, moe_w_down):
    b, s, d = x.shape
    assert b == 1 and d == D_MODEL and s % 1024 == 0
    xf = x.reshape(s, d)
    memf = mem.reshape(mem.shape[1], d)
    cos4, sin4 = _rope_tables4(s)
    bias_tab = _bias_table(rel_bias)
    nch = s // SSM_CHUNK
    g_n, n_in = SSM_GROUPS, SSM_GROUP_IN
    for l in range(DEPTH):
        pa, pb, pc, pd = _norm_proj(xf, ln_mix_g[l], _perm_w_in(w_in[l]).astype(BF16), P_SEGS, 256)
        o_a = _swa(pa, swa_qn_g[l], swa_kn_g[l], swa_sinks[l], bias_tab)
        o_b = _conv(pb, conv_dw_w[l], conv_dw_b[l], conv_ln_g[l], conv_ln_b[l], conv_pw_w[l].astype(BF16),
                    conv_pw_b[l])
        q, k, v = _mla_proj(pc, cos4, sin4, mla_qa_g[l], mla_kva_g[l],
                            _perm_heads(mla_wq_b[l], (MLA_NOPE, MLA_ROPE)).astype(BF16),
                            _perm_heads(mla_wkv_b[l], (MLA_NOPE, MLA_V)).astype(BF16),
                            mla_qn_nope_g[l], jnp.tile(mla_qn_rope_g[l], MLA_HEADS), mla_kn_nope_g[l],
                            jnp.concatenate([mla_kn_rope_g[l], jnp.zeros((64,), F32)]))
        vt = v.reshape(MLA_HEADS, s // FLASH_BK, FLASH_BK, MLA_V).swapaxes(2, 3)
        o_c = _flash(q, k, vt)
        u_r = pd.astype(BF16).reshape(nch, SSM_CHUNK, g_n, n_in).transpose(2, 0, 1, 3).reshape(g_n, nch, -1)
        y_r = _ssm(u_r, ssm_a_re[l], ssm_a_im[l], ssm_log_dt[l], ssm_b_re[l], ssm_b_im[l], ssm_c_re[l],
                   ssm_c_im[l], ssm_d[l])
        y_d = y_r.reshape(g_n, nch, SSM_CHUNK, n_in).transpose(1, 2, 0, 3).reshape(s, GROUP_WIDTH)
        xf = _mix_out(xf, o_a, o_b, o_c, y_d, ssm_glu_w[l].astype(BF16), ssm_glu_b[l], mix_out_g[l],
                      w_out[l].astype(BF16))
        (kv_mem,) = _norm_proj(memf, mem_ln_g[l], jnp.concatenate([xa_wk[l], xa_wv[l]], axis=-1).astype(BF16),
                               (2 * XA_HEADS * XA_HEAD_DIM,), memf.shape[0])
        xf = _xattn(xf, ln_cross_g[l], xa_wq[l].astype(BF16), xa_qn_g[l], kv_mem, xa_kn_g[l],
                    xa_wo[l].astype(BF16))
        xf = _moe_layer(xf, ln_moe_g[l], moe_w_group[l], moe_b_group[l], moe_w_expert[l], moe_b_expert[l],
                        moe_w_gate[l], moe_w_up[l], moe_w_down[l])
    return xf.reshape(b, s, d)
```

```python
import functools
import math

import jax
import jax.numpy as jnp
import numpy as np
from jax import lax
from jax.experimental import pallas as pl
from jax.experimental.pallas import tpu as pltpu

F32 = jnp.float32
BF16 = jnp.bfloat16

D_MODEL = 2048
DEPTH = 4
GROUP_WIDTH = 512
SWA_HEADS = 8
SWA_KV_HEADS = 2
SWA_HEAD_DIM = 64
SWA_BLOCK = 128
CONV_WIDTH = 31
MLA_HEADS = 4
MLA_NOPE = 128
MLA_ROPE = 64
MLA_V = 128
MLA_QK_PAD = 256
ROPE_THETA = 10000.0
SSM_GROUPS = 32
SSM_GROUP_IN = 16
SSM_STATE = 64
SSM_CHUNK = 64
REL_BUCKETS = 32
REL_MAX_DIST = 128
XA_HEADS = 4
XA_HEAD_DIM = 128
N_EXPERT_GROUPS = 4
EXPERTS_PER_GROUP = 8
N_EXPERTS = 32
EXPERT_FF = 512
EXPERT_BLOCK = 256
NORM_EPS = 1e-6
NEG_INF = -1e30
LOG2E = 1.4426950408889634

PA_W = 768
PB_W = 1024
PC_W = 896
PD_W = 512
P_SEGS = (PA_W, PB_W, PC_W, PD_W)

V7X_VMEM_LIMIT = 48 * 1024 * 1024

NT_DIMS = (((1,), (1,)), ((), ()))


def _cparams(sem, vmem=V7X_VMEM_LIMIT):
    return pltpu.CompilerParams(dimension_semantics=sem, vmem_limit_bytes=vmem)


def _const_spec(shape):
    nd = len(shape)
    return pl.BlockSpec(shape, lambda *_: (0,) * nd, pipeline_mode=pl.Buffered(1))


def _rms(x, g):
    return x * lax.rsqrt(jnp.mean(x * x, axis=-1, keepdims=True) + NORM_EPS) * g


def _dot(a, b):
    return jnp.dot(a, b, preferred_element_type=F32)


def _dot_nt(a, b):
    return lax.dot_general(a, b, NT_DIMS, preferred_element_type=F32)


def _norm_proj_kernel(x_ref, g_ref, w_ref, *refs, segs, n_t):
    h = _rms(x_ref[...], g_ref[...]).astype(BF16)
    o_refs = refs[1:] if n_t else refs
    off = 0
    for o_ref, n in zip(o_refs, segs):
        o_ref[...] = _dot(h, w_ref[:, off:off + n])
        off += n
    if n_t:
        o_refs[len(segs)][...] = _dot_nt(refs[0][...], h).astype(BF16)


def _norm_proj(x, g, w, segs, tm, wt=None):
    s, d = x.shape
    n = sum(segs)
    n_t = 0 if wt is None else wt.shape[0]
    in_specs = [pl.BlockSpec((tm, d), lambda i: (i, 0)), _const_spec((1, d)), _const_spec((d, n))]
    out_shape = [jax.ShapeDtypeStruct((s, n_i), F32) for n_i in segs]
    out_specs = [pl.BlockSpec((tm, n_i), lambda i: (i, 0)) for n_i in segs]
    args = [x, g.reshape(1, d), w]
    if n_t:
        in_specs.append(_const_spec((n_t, d)))
        out_shape.append(jax.ShapeDtypeStruct((n_t, s), BF16))
        out_specs.append(pl.BlockSpec((n_t, tm), lambda i: (0, i)))
        args.append(wt)
    return pl.pallas_call(
        functools.partial(_norm_proj_kernel, segs=segs, n_t=n_t),
        out_shape=out_shape,
        grid=(s // tm,),
        in_specs=in_specs,
        out_specs=out_specs,
        compiler_params=_cparams(("parallel",)),
        name="norm_proj",
    )(*args)


def _t5_buckets():
    qi = np.arange(SWA_BLOCK)[:, None]
    kj = np.arange(2 * SWA_BLOCK)[None, :]
    dist = SWA_BLOCK + qi - kj
    valid = (dist >= 0) & (dist < SWA_BLOCK)
    max_exact = REL_BUCKETS // 2
    d0 = np.maximum(dist, 0)
    dl = np.maximum(dist, 1).astype(np.float32)
    large = max_exact + (np.log(dl / max_exact) / np.log(REL_MAX_DIST / max_exact)
                         * (REL_BUCKETS - max_exact)).astype(np.int32)
    large = np.minimum(large, REL_BUCKETS - 1)
    bucket = np.where(d0 < max_exact, d0, large).astype(np.int32)
    return bucket, valid.astype(np.int32)


def _bias_kernel(rb_ref, bucket_ref, valid_ref, o_ref):
    b = bucket_ref[...]
    ok = valid_ref[...] > 0
    for h in range(SWA_HEADS):
        acc = jnp.zeros(b.shape, F32)
        for k in range(REL_BUCKETS):
            acc = jnp.where(b == k, rb_ref[k, h], acc)
        o_ref[h] = jnp.where(ok, acc, NEG_INF)


def _bias_table(rel_bias):
    bucket, valid = _t5_buckets()
    tab = pl.pallas_call(
        _bias_kernel,
        out_shape=jax.ShapeDtypeStruct((SWA_HEADS, SWA_BLOCK, 2 * SWA_BLOCK), F32),
        in_specs=[pl.BlockSpec(memory_space=pltpu.SMEM),
                  pl.BlockSpec(memory_space=pltpu.VMEM), pl.BlockSpec(memory_space=pltpu.VMEM)],
        out_specs=pl.BlockSpec(memory_space=pltpu.VMEM),
        name="t5_bias_table",
    )(rel_bias, jnp.asarray(bucket), jnp.asarray(valid))
    grp = SWA_HEADS // SWA_KV_HEADS
    tab = tab.reshape(SWA_KV_HEADS, grp, SWA_BLOCK, 2 * SWA_BLOCK).transpose(0, 3, 1, 2)
    return tab.reshape(SWA_KV_HEADS, 2 * SWA_BLOCK, grp * SWA_BLOCK)


SWA_QB = 4


def _swa_kernel(sink_ref, cur_ref, prev_ref, vt_cur_ref, vt_prev_ref, qg_ref, kg_ref, bias_ref, o_ref, ot_scr):
    i = pl.program_id(0)
    blk, dh = SWA_BLOCK, SWA_HEAD_DIM
    grp = SWA_HEADS // SWA_KV_HEADS
    scale = dh ** -0.5

    def head_rms(x, gains):
        n = x.shape[-1]
        same = (lax.broadcasted_iota(jnp.int32, (n, n), 0) // dh == lax.broadcasted_iota(jnp.int32, (n, n), 1) // dh)
        ones = same.astype(BF16)
        x2 = x * x
        hi = x2.astype(BF16)
        lo = (x2 - hi.astype(F32)).astype(BF16)
        ss = _dot(hi, ones) + _dot(lo, ones)
        return x * lax.rsqrt(ss * (1.0 / dh) + NORM_EPS) * gains

    qn = head_rms(cur_ref[:, 0:512], qg_ref[...]) * scale
    k_all = jnp.concatenate([prev_ref[:, 512:640], cur_ref[:, 512:640]], axis=0)
    kn = head_rms(k_all, kg_ref[...])
    vt_all = jnp.concatenate([vt_prev_ref[...], vt_cur_ref[...]], axis=1)
    lane_k = lax.broadcasted_iota(jnp.int32, kn.shape, 1)
    lane_q = lax.broadcasted_iota(jnp.int32, (blk, 128), 1)
    key_row = lax.broadcasted_iota(jnp.int32, (2 * blk, grp * blk), 0)
    for kh in range(SWA_KV_HEADS):
        k_one = jnp.where(lane_k // dh == kh, kn, 0.0)
        k_dup = (k_one + pltpu.roll(k_one, dh, 1)).astype(BF16)
        sink = jnp.concatenate([jnp.full((1, blk), sink_ref[kh * grp + g], F32) for g in range(grp)], axis=1)
        for j in range(SWA_QB):
            rows = slice(j * blk, (j + 1) * blk)
            qs = []
            for g in range(grp):
                h = kh * grp + g
                q_pair = qn[rows, (h // 2) * 128:(h // 2 + 1) * 128]
                qs.append(jnp.where(lane_q // dh == h % 2, q_pair, 0.0))
            q_st = jnp.concatenate(qs, axis=0).astype(BF16)
            st = _dot_nt(k_dup[j * blk:(j + 2) * blk], q_st) + bias_ref[kh]
            if j == 0:
                st = jnp.where(jnp.logical_and(i == 0, key_row < blk), NEG_INF, st)
            m = jnp.maximum(jnp.max(st, axis=0, keepdims=True), sink)
            p = jnp.exp(st - m)
            denom = jnp.sum(p, axis=0, keepdims=True) + jnp.exp(sink - m)
            ot = _dot(vt_all[kh * dh:(kh + 1) * dh, j * blk:(j + 2) * blk], p.astype(BF16)) / denom
            for g in range(grp):
                h = kh * grp + g
                ot_scr[h * dh:(h + 1) * dh, rows] = ot[:, g * blk:(g + 1) * blk]
    o_ref[...] = ot_scr[...].T


def _swa(pa, vt, qn_g, kn_g, sinks, bias_t):
    s = pa.shape[0]
    tq = SWA_QB * SWA_BLOCK
    grp = SWA_HEADS // SWA_KV_HEADS
    prev = lambda i, sk: (jnp.maximum(i * SWA_QB - 1, 0), 0)
    return pl.pallas_call(
        _swa_kernel,
        out_shape=jax.ShapeDtypeStruct((s, GROUP_WIDTH), F32),
        grid_spec=pltpu.PrefetchScalarGridSpec(
            num_scalar_prefetch=1,
            grid=(s // tq,),
            in_specs=[pl.BlockSpec((tq, PA_W), lambda i, sk: (i, 0)),
                      pl.BlockSpec((SWA_BLOCK, PA_W), prev),
                      pl.BlockSpec((128, tq), lambda i, sk: (0, i)),
                      pl.BlockSpec((128, SWA_BLOCK), lambda i, sk: (0, jnp.maximum(i * SWA_QB - 1, 0))),
                      _const_spec((1, GROUP_WIDTH)), _const_spec((1, 128)),
                      _const_spec((SWA_KV_HEADS, 2 * SWA_BLOCK, grp * SWA_BLOCK))],
            out_specs=pl.BlockSpec((tq, GROUP_WIDTH), lambda i, sk: (i, 0)),
            scratch_shapes=[pltpu.VMEM((GROUP_WIDTH, tq), F32)],
        ),
        compiler_params=_cparams(("parallel",)),
        name="swa_attention",
    )(sinks, pa, pa, vt, vt, jnp.tile(qn_g, SWA_HEADS).reshape(1, -1), jnp.tile(kn_g, SWA_KV_HEADS).reshape(1, -1),
      bias_t)


CONV_TILE = 256
CONV_HALO = 32


def _conv_kernel(cur_ref, prev_ref, dww_ref, dwb_ref, lng_ref, lnb_ref, pww_ref, pwb_ref, o_ref, z_scr):
    i = pl.program_id(0)
    t, c = CONV_TILE, GROUP_WIDTH
    z_scr[CONV_HALO:CONV_HALO + t, :] = cur_ref[:, 0:c] * jax.nn.sigmoid(cur_ref[:, c:2 * c])
    zp = prev_ref[:, 0:c] * jax.nn.sigmoid(prev_ref[:, c:2 * c])
    z_scr[0:CONV_HALO, :] = jnp.where(i > 0, zp, 0.0)
    base = CONV_HALO - (CONV_WIDTH - 1)
    acc = jnp.zeros((t, c), F32)
    for k in range(CONV_WIDTH):
        acc = acc + dww_ref[k:k + 1, :] * z_scr[base + k:base + k + t, :]
    acc = acc + dwb_ref[...]
    xc = acc - jnp.mean(acc, axis=-1, keepdims=True)
    y = xc * lax.rsqrt(jnp.mean(xc * xc, axis=-1, keepdims=True) + NORM_EPS) * lng_ref[...] + lnb_ref[...]
    y = y * jax.nn.sigmoid(y)
    o_ref[...] = _dot(y.astype(BF16), pww_ref[...]) + pwb_ref[...]


def _conv(pb, dw_w, dw_b, ln_g, ln_b, pw_w, pw_b):
    s = pb.shape[0]
    c = GROUP_WIDTH
    r = CONV_TILE // CONV_HALO
    row = lambda v: v.reshape(1, c)
    return pl.pallas_call(
        _conv_kernel,
        out_shape=jax.ShapeDtypeStruct((s, c), F32),
        grid=(s // CONV_TILE,),
        in_specs=[pl.BlockSpec((CONV_TILE, PB_W), lambda i: (i, 0)),
                  pl.BlockSpec((CONV_HALO, PB_W), lambda i: (jnp.maximum(i * r - 1, 0), 0)),
                  _const_spec((CONV_WIDTH, c)), _const_spec((1, c)), _const_spec((1, c)), _const_spec((1, c)),
                  _const_spec((c, c)), _const_spec((1, c))],
        out_specs=pl.BlockSpec((CONV_TILE, c), lambda i: (i, 0)),
        scratch_shapes=[pltpu.VMEM((CONV_TILE + CONV_HALO, c), F32)],
        compiler_params=_cparams(("parallel",)),
        name="conformer_conv",
    )(pb, pb, dw_w, row(dw_b), row(ln_g), row(ln_b), pw_w, row(pw_b))


MLA_TILE = 256


def _rot_half64(x):
    n = x.shape[-1]
    ax = x.ndim - 1
    lane = lax.broadcasted_iota(jnp.int32, x.shape, ax)
    return jnp.where((lane % 64) < 32, -pltpu.roll(x, n - 32, ax), pltpu.roll(x, 32, ax))


def _seg_rms64(x):
    n = x.shape[-1]
    lane = lax.broadcasted_iota(jnp.int32, x.shape, x.ndim - 1)
    x2 = x * x
    r = jnp.zeros(x.shape, F32)
    for h in range(n // 64):
        msk = jnp.logical_and(lane >= h * 64, lane < (h + 1) * 64)
        sm = jnp.sum(jnp.where(msk, x2, 0.0), axis=-1, keepdims=True)
        r = jnp.where(msk, lax.rsqrt(sm * (1.0 / 64) + NORM_EPS), r)
    return x * r


def _mla_proj_kernel(pc_ref, cos_ref, sin_ref, qag_ref, kvag_ref, wq_ref, wkv_ref,
                     qng_ref, qrg_ref, kng_ref, krg_ref, q_ref, k_ref, v_ref):
    h_n, dn = MLA_HEADS, MLA_NOPE
    scale = (MLA_NOPE + MLA_ROPE) ** -0.5 * LOG2E
    cos4 = cos_ref[...]
    sin4 = sin_ref[...]
    qn = _rms(pc_ref[:, 0:512], qag_ref[...]).astype(BF16)
    q = _dot(qn, wq_ref[...])
    kvn = _rms(pc_ref[:, 512:768], kvag_ref[...]).astype(BF16)
    kv = _dot(kvn, wkv_ref[...])
    qr = _seg_rms64(q[:, h_n * dn:]) * qrg_ref[...]
    qr = (qr * cos4 + _rot_half64(qr) * sin4) * scale
    kr = pc_ref[:, 768:896]
    kr = kr * lax.rsqrt(jnp.sum(kr * kr, axis=-1, keepdims=True) * (1.0 / MLA_ROPE) + NORM_EPS) * krg_ref[...]
    kr = (kr * cos4[:, 0:128] + _rot_half64(kr) * sin4[:, 0:128]).astype(BF16)
    lane = lax.broadcasted_iota(jnp.int32, (q.shape[0], 128), 1)
    for h in range(h_n):
        qnope = _rms(q[:, h * dn:(h + 1) * dn], qng_ref[...]) * scale
        blk = qr[:, (h // 2) * 128:(h // 2 + 1) * 128]
        if h % 2 == 1:
            blk = pltpu.roll(blk, 64, 1)
        blk = jnp.where(lane < 64, blk, 0.0)
        q_ref[h] = jnp.concatenate([qnope, blk], axis=-1).astype(BF16)
        knope = _rms(kv[:, h * dn:(h + 1) * dn], kng_ref[...]).astype(BF16)
        k_ref[h] = jnp.concatenate([knope, kr], axis=-1)
        v_ref[h] = kv[:, (h_n + h) * dn:(h_n + h + 1) * dn].astype(BF16)


def _mla_proj(pc, cos4, sin4, qa_g, kva_g, wq, wkv, qn_nope_g, qn_rope_g4, kn_nope_g, kn_rope_g_pad):
    s = pc.shape[0]
    t = MLA_TILE
    h = MLA_HEADS
    row = lambda v: v.reshape(1, -1)
    return pl.pallas_call(
        _mla_proj_kernel,
        out_shape=[jax.ShapeDtypeStruct((h, s, MLA_QK_PAD), BF16),
                   jax.ShapeDtypeStruct((h, s, MLA_QK_PAD), BF16),
                   jax.ShapeDtypeStruct((h, s, MLA_V), BF16)],
        grid=(s // t,),
        in_specs=[pl.BlockSpec((t, PC_W), lambda i: (i, 0)),
                  pl.BlockSpec((t, 256), lambda i: (i, 0)), pl.BlockSpec((t, 256), lambda i: (i, 0)),
                  _const_spec((1, 512)), _const_spec((1, 256)),
                  _const_spec((512, 768)), _const_spec((256, 1024)),
                  _const_spec((1, 128)), _const_spec((1, 256)), _const_spec((1, 128)), _const_spec((1, 128))],
        out_specs=[pl.BlockSpec((h, t, MLA_QK_PAD), lambda i: (0, i, 0)),
                   pl.BlockSpec((h, t, MLA_QK_PAD), lambda i: (0, i, 0)),
                   pl.BlockSpec((h, t, MLA_V), lambda i: (0, i, 0))],
        compiler_params=_cparams(("parallel",)),
        name="mla_proj",
    )(pc, cos4, sin4, row(qa_g), row(kva_g), wq, wkv, row(qn_nope_g), row(qn_rope_g4), row(kn_nope_g),
      row(kn_rope_g_pad))


FLASH_BQ = 1024
FLASH_BK = 512


def _flash_kernel(q_ref, k_ref, vt_ref, o_ref, sa_scr, sb_scr, m_scr, l_scr, acc_scr):
    bq, bk = FLASH_BQ, FLASH_BK
    assert bq == 2 * bk
    qi = pl.program_id(1)
    m_scr[...] = jnp.full(m_scr.shape, NEG_INF, F32)
    l_scr[...] = jnp.zeros(l_scr.shape, F32)
    acc_scr[...] = jnp.zeros(acc_scr.shape, F32)

    def scores(j, s_scr):
        k = k_ref[pl.ds(pl.multiple_of(j * bk, bk), bk), :]
        s_scr[...] = _dot_nt(k, q_ref[...])

    def update(j, s_scr, masked):
        st = s_scr[...]
        if masked:
            kpos = j * bk + lax.broadcasted_iota(jnp.int32, (bk, bq), 0)
            qpos = qi * bq + lax.broadcasted_iota(jnp.int32, (bk, bq), 1)
            st = jnp.where(kpos <= qpos, st, NEG_INF)
        m_prev = m_scr[...]
        m_new = jnp.maximum(m_prev, jnp.max(st, axis=0, keepdims=True))
        alpha = jnp.exp2(m_prev - m_new)
        p = jnp.exp2(st - m_new)
        l_scr[...] = alpha * l_scr[...] + jnp.sum(p, axis=0, keepdims=True)
        acc_scr[...] = alpha * acc_scr[...] + _dot(vt_ref[j], p.astype(BF16))
        m_scr[...] = m_new

    scores(0, sa_scr)

    def body(i, carry):
        scores(2 * i + 1, sb_scr)
        update(2 * i, sa_scr, False)
        scores(2 * i + 2, sa_scr)
        update(2 * i + 1, sb_scr, False)
        return carry

    lax.fori_loop(0, qi, body, 0)
    scores(2 * qi + 1, sb_scr)
    update(2 * qi, sa_scr, True)
    update(2 * qi + 1, sb_scr, True)
    o_ref[...] = (acc_scr[...] / l_scr[...]).T


def _flash(q, k, vt):
    h, s, _ = q.shape
    nkb = s // FLASH_BK
    return pl.pallas_call(
        _flash_kernel,
        out_shape=jax.ShapeDtypeStruct((s, h * MLA_V), F32),
        grid=(h, s // FLASH_BQ),
        in_specs=[pl.BlockSpec((None, FLASH_BQ, MLA_QK_PAD), lambda hh, i: (hh, i, 0)),
                  pl.BlockSpec((None, s, MLA_QK_PAD), lambda hh, i: (hh, 0, 0)),
                  pl.BlockSpec((None, nkb, MLA_V, FLASH_BK), lambda hh, i: (hh, 0, 0, 0))],
        out_specs=pl.BlockSpec((FLASH_BQ, MLA_V), lambda hh, i: (i, hh)),
        scratch_shapes=[pltpu.VMEM((FLASH_BK, FLASH_BQ), F32), pltpu.VMEM((FLASH_BK, FLASH_BQ), F32),
                        pltpu.VMEM((1, FLASH_BQ), F32), pltpu.VMEM((1, FLASH_BQ), F32),
                        pltpu.VMEM((MLA_V, FLASH_BQ), F32)],
        compiler_params=_cparams(("parallel", "arbitrary")),
        name="mla_flash",
    )(q, k, vt)


def _split3_dot(a, sel):
    a1 = a.astype(BF16)
    r1 = a - a1.astype(F32)
    a2 = r1.astype(BF16)
    a3 = (r1 - a2.astype(F32)).astype(BF16)
    return _dot(a1, sel) + _dot(a2, sel) + _dot(a3, sel)


def _hi_dot(a, b):
    return lax.dot_general(a, b, (((1,), (0,)), ((), ())), precision=lax.Precision.HIGHEST,
                           preferred_element_type=F32)


def _ssm_kernel(u_ref, arc_ref, aic_ref, arr_ref, air_ref, ldt_ref, bre_ref, bim_ref, btre_ref, btim_ref,
                ctre_ref, ctim_ref, dcol_ref, y_ref, toep_scr, *, nch):
    tc, n_in, n_st = SSM_CHUNK, SSM_GROUP_IN, SSM_STATE
    w = tc * n_in
    dt = jnp.exp(ldt_ref[...])
    ar_c, ai_c = arc_ref[...], aic_ref[...]
    ar_r, ai_r = arr_ref[...], air_ref[...]

    def lam_pow(ar, ai, e):
        mag = jnp.exp(ar * dt * e)
        ang = ai * dt * e
        return mag * jnp.cos(ang), mag * jnp.sin(ang)

    def zoh_coef(ar, ai):
        lr, li = lam_pow(ar, ai, 1.0)
        x, y = lr - 1.0, li
        den = ar * ar + ai * ai
        return (x * ar + y * ai) / den, (y * ar - x * ai) / den

    lane_w = lax.broadcasted_iota(jnp.int32, (tc, w), 1)
    sel_tau = (lane_w // n_in == lax.broadcasted_iota(jnp.int32, (tc, w), 0)).astype(BF16)
    lane_o = lax.broadcasted_iota(jnp.int32, (n_in, w), 1)
    sel_o = (lane_o % n_in == lax.broadcasted_iota(jnp.int32, (n_in, w), 0)).astype(BF16)

    tau = lax.broadcasted_iota(jnp.int32, (n_st, tc), 1).astype(F32)
    lre, lim = lam_pow(ar_c, ai_c, tau)
    lre_x, lim_x = _split3_dot(lre, sel_tau), _split3_dot(lim, sel_tau)
    rre, rim = lam_pow(ar_c, ai_c, (tc - 1.0) - tau)
    rre_x, rim_x = _split3_dot(rre, sel_tau), _split3_dot(rim, sel_tau)
    ctre_x, ctim_x = _split3_dot(ctre_ref[...], sel_o), _split3_dot(ctim_ref[...], sel_o)

    e_re = lre_x * ctre_x - lim_x * ctim_x
    e_im = lre_x * ctim_x + lim_x * ctre_x

    cr_c, ci_c = zoh_coef(ar_c, ai_c)
    bb_re = cr_c * bre_ref[...] - ci_c * bim_ref[...]
    bb_im = cr_c * bim_ref[...] + ci_c * bre_ref[...]
    bb_re_x, bb_im_x = _split3_dot(bb_re, sel_o), _split3_dot(bb_im, sel_o)
    f_re = (rre_x * bb_re_x - rim_x * bb_im_x).astype(BF16)
    f_im = (rre_x * bb_im_x + rim_x * bb_re_x).astype(BF16)

    cr_r, ci_r = zoh_coef(ar_r, ai_r)
    bbt_re = cr_r * btre_ref[...] - ci_r * btim_ref[...]
    bbt_im = cr_r * btim_ref[...] + ci_r * btre_ref[...]
    g_row = _hi_dot(bbt_re, e_re) - _hi_dot(bbt_im, e_im)
    lane_g = lax.broadcasted_iota(jnp.int32, (n_in, w), 1)
    row_g = lax.broadcasted_iota(jnp.int32, (n_in, w), 0)
    g_row = g_row + jnp.where(lane_g == row_g, dcol_ref[...], 0.0)
    for s in range(tc):
        rolled = g_row if s == 0 else pltpu.roll(g_row, n_in * s, 1)
        toep_scr[s * n_in:(s + 1) * n_in, :] = jnp.where(lane_g >= n_in * s, rolled, 0.0).astype(BF16)

    u = u_ref[...]
    s_re = _dot_nt(u, f_re)
    s_im = _dot_nt(u, f_im)
    row = lax.broadcasted_iota(jnp.int32, (nch, n_st), 0)
    sh = 1
    while sh < nch:
        pr, pi = lam_pow(ar_r, ai_r, float(tc * sh))
        re_s = jnp.where(row >= sh, pltpu.roll(s_re, sh, 0), 0.0)
        im_s = jnp.where(row >= sh, pltpu.roll(s_im, sh, 0), 0.0)
        s_re, s_im = s_re + pr * re_s - pi * im_s, s_im + pr * im_s + pi * re_s
        sh *= 2
    p_re = jnp.where(row >= 1, pltpu.roll(s_re, 1, 0), 0.0)
    p_im = jnp.where(row >= 1, pltpu.roll(s_im, 1, 0), 0.0)
    l1r, l1i = lam_pow(ar_c, ai_c, 1.0)
    h_re = (l1r * e_re - l1i * e_im).astype(BF16)
    h_im = (l1r * e_im + l1i * e_re).astype(BF16)
    y = _dot(u, toep_scr[...]) + _dot(p_re.astype(BF16), h_re) - _dot(p_im.astype(BF16), h_im)
    y_ref[...] = y


def _ssm(u_r, a_re, a_im, log_dt, b_re, b_im, c_re, c_im, d_skip):
    g, nch, w = u_r.shape
    n_st, n_in = SSM_STATE, SSM_GROUP_IN
    col = lambda v: v.reshape(g, n_st, 1)
    rw = lambda v: v.reshape(g, 1, n_st)
    spec = lambda *shape: pl.BlockSpec((None,) + shape, lambda i: (i,) + (0,) * len(shape))
    return pl.pallas_call(
        functools.partial(_ssm_kernel, nch=nch),
        out_shape=jax.ShapeDtypeStruct((g, nch, w), F32),
        grid=(g,),
        in_specs=[spec(nch, w), spec(n_st, 1), spec(n_st, 1), spec(1, n_st), spec(1, n_st), spec(1, 1),
                  spec(n_st, n_in), spec(n_st, n_in), spec(n_in, n_st), spec(n_in, n_st),
                  spec(n_st, n_in), spec(n_st, n_in), spec(n_in, 1)],
        out_specs=spec(nch, w),
        scratch_shapes=[pltpu.VMEM((w, w), BF16)],
        compiler_params=_cparams(("parallel",)),
        name="s5_ssm",
    )(u_r, col(a_re), col(a_im), rw(a_re), rw(a_im), log_dt.reshape(g, 1, 1), b_re, b_im,
      jnp.swapaxes(b_re, 1, 2), jnp.swapaxes(b_im, 1, 2), jnp.swapaxes(c_re, 1, 2), jnp.swapaxes(c_im, 1, 2),
      d_skip.reshape(g, n_in, 1))


OUT_TILE = 256


def _mix_out_kernel(x_ref, oa_ref, ob_ref, oc_ref, yd_ref, gw_ref, gb_ref, mg_ref, wo_ref, o_ref):
    c = GROUP_WIDTH
    y = jax.nn.gelu(yd_ref[...])
    od = y * jax.nn.sigmoid(_dot(y.astype(BF16), gw_ref[...]) + gb_ref[...])
    acc = x_ref[...]
    for m, o in enumerate((oa_ref[...], ob_ref[...], oc_ref[...], od)):
        n = _rms(o, mg_ref[:, m * c:(m + 1) * c]).astype(BF16)
        acc = acc + _dot(n, wo_ref[m * c:(m + 1) * c, :])
    o_ref[...] = acc


def _mix_out(x, oa, ob, oc, yd, glu_w, glu_b, mix_g, w_out):
    s, d = x.shape
    c = GROUP_WIDTH
    t = OUT_TILE
    tile = lambda n: pl.BlockSpec((t, n), lambda i: (i, 0))
    return pl.pallas_call(
        _mix_out_kernel,
        out_shape=jax.ShapeDtypeStruct((s, d), F32),
        grid=(s // t,),
        in_specs=[tile(d), tile(c), tile(c), tile(c), tile(c),
                  _const_spec((c, c)), _const_spec((1, c)), _const_spec((1, d)), _const_spec((d, d))],
        out_specs=tile(d),
        compiler_params=_cparams(("parallel",)),
        name="mix_out",
    )(x, oa, ob, oc, yd, glu_w, glu_b.reshape(1, c), mix_g.reshape(1, d), w_out)


XA_TILE = 256


def _xattn_kernel(x_ref, lg_ref, wq_ref, qg_ref, kv_ref, kg_ref, wo_ref, o_ref):
    dh = XA_HEAD_DIM
    hw = XA_HEADS * dh
    x = x_ref[...]
    q = _dot(_rms(x, lg_ref[...]).astype(BF16), wq_ref[...])
    scale = dh ** -0.5
    acc = x
    for h in range(XA_HEADS):
        qh = (_rms(q[:, h * dh:(h + 1) * dh], qg_ref[...]) * scale).astype(BF16)
        kh = _rms(kv_ref[:, h * dh:(h + 1) * dh], kg_ref[...]).astype(BF16)
        vh = kv_ref[:, hw + h * dh:hw + (h + 1) * dh].astype(BF16)
        logits = _dot_nt(qh, kh)
        p = jnp.exp(logits - jnp.max(logits, axis=-1, keepdims=True))
        oh = _dot(p.astype(BF16), vh) / jnp.sum(p, axis=-1, keepdims=True)
        acc = acc + _dot(oh.astype(BF16), wo_ref[h * dh:(h + 1) * dh, :])
    o_ref[...] = acc


def _xattn(x, ln_g, wq, qn_g, kv, kn_g, wo):
    s, d = x.shape
    m, kvw = kv.shape
    hw = XA_HEADS * XA_HEAD_DIM
    t = XA_TILE
    return pl.pallas_call(
        _xattn_kernel,
        out_shape=jax.ShapeDtypeStruct((s, d), F32),
        grid=(s // t,),
        in_specs=[pl.BlockSpec((t, d), lambda i: (i, 0)), _const_spec((1, d)), _const_spec((d, hw)),
                  _const_spec((1, XA_HEAD_DIM)), _const_spec((m, kvw)), _const_spec((1, XA_HEAD_DIM)),
                  _const_spec((hw, d))],
        out_specs=pl.BlockSpec((t, d), lambda i: (i, 0)),
        compiler_params=_cparams(("parallel",)),
        name="mem_xattn",
    )(x, ln_g.reshape(1, d), wq, qn_g.reshape(1, -1), kv, kn_g.reshape(1, -1), wo)


ROUTER_TILE = 256
ROUTER_LANES = 128


def _router_kernel(x_ref, lg_ref, whi_ref, wlo_ref, b_ref, h_ref, r_ref, cnt_ref, carry_scr):
    i = pl.program_id(0)
    t = ROUTER_TILE
    ng, epg = N_EXPERT_GROUPS, EXPERTS_PER_GROUP

    @pl.when(i == 0)
    def _():
        carry_scr[...] = jnp.zeros(carry_scr.shape, F32)

    h = _rms(x_ref[...], lg_ref[...])
    h_hi = h.astype(BF16)
    h_lo = (h - h_hi.astype(F32)).astype(BF16)
    h_ref[...] = _pack_bf16_pair(h[:, 0:D_MODEL // 2], h[:, D_MODEL // 2:])
    logits = _dot(h_hi, whi_ref[...]) + _dot(h_hi, wlo_ref[...]) + _dot(h_lo, whi_ref[...]) + b_ref[...]
    lane = lax.broadcasted_iota(jnp.int32, (t, ROUTER_LANES), 1)
    big = jnp.int32(ROUTER_LANES)

    def first_max(vals, mask):
        v = jnp.where(mask, vals, -jnp.inf)
        mx = jnp.max(v, axis=-1, keepdims=True)
        idx = jnp.min(jnp.where(jnp.logical_and(mask, v == mx), lane, big), axis=-1, keepdims=True)
        return mx, idx

    gmask = lane < ng
    gmax, gidx = first_max(logits, gmask)
    g_w = 1.0 / jnp.sum(jnp.where(gmask, jnp.exp(logits - gmax), 0.0), axis=-1, keepdims=True)
    emask = jnp.logical_and(lane >= ng + gidx * epg, lane < ng + (gidx + 1) * epg)
    v1, i1 = first_max(logits, emask)
    v2, i2 = first_max(logits, jnp.logical_and(emask, lane != i1))
    e2 = jnp.exp(v2 - v1)
    gate1 = g_w / (1.0 + e2)
    gate2 = g_w * e2 / (1.0 + e2)

    oh1 = lane == i1
    oh2 = lane == i2
    both = (oh1.astype(F32) + oh2.astype(F32))
    tri = (lax.broadcasted_iota(jnp.int32, (t, t), 1) < lax.broadcasted_iota(jnp.int32, (t, t), 0)).astype(BF16)
    before = _dot(tri, both.astype(BF16)) + carry_scr[...]
    rank1 = jnp.sum(jnp.where(oh1, before, 0.0), axis=-1, keepdims=True)
    rank2 = jnp.sum(jnp.where(oh2, before, 0.0), axis=-1, keepdims=True)
    carry_scr[...] = carry_scr[...] + jnp.sum(both, axis=0, keepdims=True)
    cnt_ref[...] = jnp.broadcast_to(carry_scr[...], cnt_ref.shape)

    out = jnp.zeros((t, ROUTER_LANES), F32)
    for k, val in enumerate(((i1 - ng).astype(F32), (i2 - ng).astype(F32), gate1, gate2, rank1, rank2)):
        out = jnp.where(lane == k, val, out)
    r_ref[...] = out


def _router(x, ln_g, w_hi, w_lo, bias):
    s, d = x.shape
    t = ROUTER_TILE
    return pl.pallas_call(
        _router_kernel,
        out_shape=[jax.ShapeDtypeStruct((s, d // 2), jnp.uint32), jax.ShapeDtypeStruct((s, ROUTER_LANES), F32),
                   jax.ShapeDtypeStruct((8, ROUTER_LANES), F32)],
        grid=(s // t,),
        in_specs=[pl.BlockSpec((t, d), lambda i: (i, 0)), _const_spec((1, d)),
                  _const_spec((d, ROUTER_LANES)), _const_spec((d, ROUTER_LANES)), _const_spec((1, ROUTER_LANES))],
        out_specs=[pl.BlockSpec((t, d // 2), lambda i: (i, 0)), pl.BlockSpec((t, ROUTER_LANES), lambda i: (i, 0)),
                   pl.BlockSpec((8, ROUTER_LANES), lambda i: (0, 0))],
        scratch_shapes=[pltpu.VMEM((1, ROUTER_LANES), F32)],
        compiler_params=_cparams(("arbitrary",)),
        name="moe_router",
    )(x, ln_g.reshape(1, d), w_hi, w_lo, bias)


def _ffn_kernel(be_ref, nu_ref, x_ref, wg_ref, wu_ref, wd_ref, o_ref, wg_scr, wu_scr, wd_scr):
    b = pl.program_id(0)
    half = D_MODEL // 2
    prev = be_ref[jnp.maximum(b - 1, 0)]

    @pl.when(jnp.logical_or(b == 0, be_ref[b] != prev))
    def _():
        wg_scr[...] = wg_ref[...].astype(BF16)
        wu_scr[...] = wu_ref[...].astype(BF16)
        wd_scr[...] = wd_ref[...].astype(BF16)

    @pl.when(b < nu_ref[0])
    def _():
        xa, xb = _unpack_bf16_pair(x_ref[...])
        xa, xb = xa.astype(BF16), xb.astype(BF16)
        g = _dot(xa, wg_scr[0:half, :]) + _dot(xb, wg_scr[half:, :])
        u = _dot(xa, wu_scr[0:half, :]) + _dot(xb, wu_scr[half:, :])
        hid = ((g * jax.nn.sigmoid(g)) * u).astype(BF16)
        o_ref[...] = _pack_bf16_pair(_dot(hid, wd_scr[:, 0:half]), _dot(hid, wd_scr[:, half:]))

    @pl.when(b >= nu_ref[0])
    def _():
        o_ref[...] = jnp.zeros(o_ref.shape, o_ref.dtype)


def _ffn(xs, block_expert, n_used, wg, wu, wd, layer):
    p, half = xs.shape
    d = 2 * half
    nb = p // EXPERT_BLOCK
    ff = EXPERT_FF
    return pl.pallas_call(
        _ffn_kernel,
        out_shape=jax.ShapeDtypeStruct((p, half), jnp.uint32),
        grid_spec=pltpu.PrefetchScalarGridSpec(
            num_scalar_prefetch=2,
            grid=(nb,),
            in_specs=[pl.BlockSpec((EXPERT_BLOCK, half), lambda b, be, nu: (b, 0)),
                      pl.BlockSpec((None, None, d, ff), lambda b, be, nu: (layer, be[b], 0, 0)),
                      pl.BlockSpec((None, None, d, ff), lambda b, be, nu: (layer, be[b], 0, 0)),
                      pl.BlockSpec((None, None, ff, d), lambda b, be, nu: (layer, be[b], 0, 0))],
            out_specs=pl.BlockSpec((EXPERT_BLOCK, half), lambda b, be, nu: (b, 0)),
            scratch_shapes=[pltpu.VMEM((d, ff), BF16), pltpu.VMEM((d, ff), BF16), pltpu.VMEM((ff, d), BF16)],
        ),
        compiler_params=_cparams(("arbitrary",), vmem=56 * 1024 * 1024),
        name="moe_ffn",
    )(block_expert, n_used, xs, wg, wu, wd)


MOE_ROW_TILE = 256


def _pack_bf16_pair(a, b):
    hi = lax.bitcast_convert_type(a.astype(BF16).astype(F32), jnp.uint32)
    lo = lax.bitcast_convert_type(b.astype(BF16).astype(F32), jnp.uint32)
    return hi | (lo >> 16)


def _unpack_bf16_pair(w):
    a = lax.bitcast_convert_type(w & jnp.uint32(0xFFFF0000), F32)
    b = lax.bitcast_convert_type(w << 16, F32)
    return a, b


def _dispatch_kernel(dest_ref, h_ref, init_hbm, xs_hbm, sem):
    del init_hbm
    t = MOE_ROW_TILE

    def row_copy(r, k):
        return pltpu.make_async_copy(h_ref.at[pl.ds(r, 1)], xs_hbm.at[pl.ds(dest_ref[0, 0, 2 * r + k], 1)], sem)

    def issue(r, carry):
        row_copy(r, 0).start()
        row_copy(r, 1).start()
        return carry

    def drain(r, carry):
        row_copy(r, 0).wait()
        row_copy(r, 1).wait()
        return carry

    lax.fori_loop(0, t, issue, 0, unroll=8)
    lax.fori_loop(0, t, drain, 0, unroll=8)


def _dispatch(h_packed, dest, n_rows):
    s, half = h_packed.shape
    t = MOE_ROW_TILE
    return pl.pallas_call(
        _dispatch_kernel,
        out_shape=jax.ShapeDtypeStruct((n_rows, half), jnp.uint32),
        grid=(s // t,),
        in_specs=[pl.BlockSpec((1, 1, 2 * t), lambda i: (i, 0, 0), memory_space=pltpu.SMEM),
                  pl.BlockSpec((t, half), lambda i: (i, 0)), pl.BlockSpec(memory_space=pl.ANY)],
        out_specs=pl.BlockSpec(memory_space=pl.ANY),
        scratch_shapes=[pltpu.SemaphoreType.DMA(())],
        input_output_aliases={2: 0},
        compiler_params=_cparams(("arbitrary",)),
        name="moe_dispatch",
    )(dest.reshape(s // t, 1, 2 * t), h_packed, jnp.zeros((n_rows, half), jnp.uint32))


def _combine_kernel(dest_ref, x_ref, route_ref, rows_hbm, o_ref, buf, sem):
    t = MOE_ROW_TILE
    half = D_MODEL // 2

    def row_copy(r, k):
        return pltpu.make_async_copy(rows_hbm.at[pl.ds(dest_ref[0, 0, 2 * r + k], 1)], buf.at[pl.ds(k * t + r, 1)], sem)

    def issue(r, carry):
        row_copy(r, 0).start()
        row_copy(r, 1).start()
        return carry

    def drain(r, carry):
        row_copy(r, 0).wait()
        row_copy(r, 1).wait()
        return carry

    lax.fori_loop(0, t, issue, 0, unroll=8)
    lax.fori_loop(0, t, drain, 0, unroll=8)
    g0 = route_ref[:, 2:3]
    g1 = route_ref[:, 3:4]
    a0, b0 = _unpack_bf16_pair(buf[0:t, :])
    a1, b1 = _unpack_bf16_pair(buf[t:2 * t, :])
    o_ref[:, 0:half] = x_ref[:, 0:half] + g0 * a0 + g1 * a1
    o_ref[:, half:] = x_ref[:, half:] + g0 * b0 + g1 * b1


def _combine(x, route, rows, dest):
    s, d = x.shape
    t = MOE_ROW_TILE
    return pl.pallas_call(
        _combine_kernel,
        out_shape=jax.ShapeDtypeStruct((s, d), F32),
        grid=(s // t,),
        in_specs=[pl.BlockSpec((1, 1, 2 * t), lambda i: (i, 0, 0), memory_space=pltpu.SMEM),
                  pl.BlockSpec((t, d), lambda i: (i, 0)), pl.BlockSpec((t, ROUTER_LANES), lambda i: (i, 0)),
                  pl.BlockSpec(memory_space=pl.ANY)],
        out_specs=pl.BlockSpec((t, d), lambda i: (i, 0)),
        scratch_shapes=[pltpu.VMEM((2 * t, d // 2), jnp.uint32), pltpu.SemaphoreType.DMA(())],
        compiler_params=_cparams(("arbitrary",)),
        name="moe_combine",
    )(dest.reshape(s // t, 1, 2 * t), x, route, rows)


def _rope_tables4(s):
    inv = 1.0 / (ROPE_THETA ** (jnp.arange(0, MLA_ROPE, 2, dtype=F32) / MLA_ROPE))
    ang = jnp.arange(s, dtype=F32)[:, None] * inv[None, :]
    cos, sin = jnp.cos(ang), jnp.sin(ang)
    return jnp.tile(cos, (1, 8)), jnp.tile(sin, (1, 8))


def _moe_layer(x, ln_g, w_group, b_group, w_expert, b_expert, wg, wu, wd, layer):
    s, d = x.shape
    ng, ne = N_EXPERT_GROUPS, N_EXPERTS
    w_r = jnp.zeros((d, ROUTER_LANES), F32).at[:, :ng].set(w_group).at[:, ng:ng + ne].set(w_expert)
    w_hi = w_r.astype(BF16)
    w_lo = (w_r - w_hi.astype(F32)).astype(BF16)
    b_r = jnp.zeros((1, ROUTER_LANES), F32).at[0, :ng].set(b_group).at[0, ng:ng + ne].set(b_expert)
    h_packed, route, cnt = _router(x, ln_g, w_hi, w_lo, b_r)
    e_id = route[:, 0:2].astype(jnp.int32)
    rank = route[:, 4:6].astype(jnp.int32)
    counts = cnt[0, ng:ng + ne].astype(jnp.int32)
    n_blocks = -(-(2 * s + ne * (EXPERT_BLOCK - 1)) // EXPERT_BLOCK)
    padded = (counts + EXPERT_BLOCK - 1) // EXPERT_BLOCK * EXPERT_BLOCK
    pad_ends = jnp.cumsum(padded)
    pad_starts = pad_ends - padded
    dest = pad_starts[e_id] + rank
    block_starts = jnp.arange(n_blocks, dtype=jnp.int32) * EXPERT_BLOCK
    block_expert = jnp.minimum(jnp.sum(pad_ends[None, :] <= block_starts[:, None], axis=1), ne - 1).astype(jnp.int32)
    n_used = (pad_ends[-1] // EXPERT_BLOCK).astype(jnp.int32).reshape(1)
    xs = _dispatch(h_packed, dest, n_blocks * EXPERT_BLOCK)
    rows = _ffn(xs, block_expert, n_used, wg, wu, wd, layer)
    return _combine(x, route, rows, dest)


def _perm_w_in(w_in):
    sp = np.cumsum((512, 128, 128, 512, 512, 512, 256, 64, 512))
    a_q, a_k, a_v, b_a, b_g, c_q, c_kv, c_kr, d_u = jnp.split(w_in, [int(v) for v in sp[:-1]], axis=-1)
    zero = jnp.zeros((w_in.shape[0], 64), w_in.dtype)
    return jnp.concatenate([a_q, a_k, a_v, b_a, b_g, c_q, c_kv, c_kr, zero, d_u], axis=-1)


def _perm_heads(w, parts):
    k = w.shape[0]
    wh = w.reshape(k, MLA_HEADS, sum(parts))
    offs = np.cumsum((0,) + tuple(parts))
    return jnp.concatenate([wh[:, :, offs[j]:offs[j + 1]].reshape(k, -1) for j in range(len(parts))], axis=-1)


def kernel(x, mem, rel_bias, ln_mix_g, w_in, swa_qn_g, swa_kn_g, swa_sinks, conv_dw_w, conv_dw_b, conv_ln_g, conv_ln_b, conv_pw_w, conv_pw_b, mla_qa_g, mla_kva_g, mla_wq_b, mla_wkv_b, mla_qn_nope_g, mla_qn_rope_g, mla_kn_nope_g, mla_kn_rope_g, ssm_a_re, ssm_a_im, ssm_log_dt, ssm_b_re, ssm_b_im, ssm_c_re, ssm_c_im, ssm_d, ssm_glu_w, ssm_glu_b, mix_out_g, w_out, ln_cross_g, mem_ln_g, xa_wq, xa_wk, xa_wv, xa_qn_g, xa_kn_g, xa_wo, ln_moe_g, moe_w_group, moe_b_group, moe_w_expert, moe_b_expert, moe_w_gate, moe_w_up, moe_w_down):
    b, s, d = x.shape
    assert b == 1 and d == D_MODEL and s % 1024 == 0
    xf = x.reshape(s, d)
    memf = mem.reshape(mem.shape[1], d)
    cos4, sin4 = _rope_tables4(s)
    bias_tab = _bias_table(rel_bias)
    nch = s // SSM_CHUNK
    g_n, n_in = SSM_GROUPS, SSM_GROUP_IN
    for l in range(DEPTH):
        pa, pb, pc, pd, vt_a = _norm_proj(xf, ln_mix_g[l], _perm_w_in(w_in[l]).astype(BF16), P_SEGS, 256,
                                          wt=w_in[l][:, 640:768].T.astype(BF16))
        o_a = _swa(pa, vt_a, swa_qn_g[l], swa_kn_g[l], swa_sinks[l], bias_tab)
        o_b = _conv(pb, conv_dw_w[l], conv_dw_b[l], conv_ln_g[l], conv_ln_b[l], conv_pw_w[l].astype(BF16),
                    conv_pw_b[l])
        q, k, v = _mla_proj(pc, cos4, sin4, mla_qa_g[l], mla_kva_g[l],
                            _perm_heads(mla_wq_b[l], (MLA_NOPE, MLA_ROPE)).astype(BF16),
                            _perm_heads(mla_wkv_b[l], (MLA_NOPE, MLA_V)).astype(BF16),
                            mla_qn_nope_g[l], jnp.tile(mla_qn_rope_g[l], MLA_HEADS), mla_kn_nope_g[l],
                            jnp.concatenate([mla_kn_rope_g[l], jnp.zeros((64,), F32)]))
        vt = v.reshape(MLA_HEADS, s // FLASH_BK, FLASH_BK, MLA_V).swapaxes(2, 3)
        o_c = _flash(q, k, vt)
        u_r = pd.astype(BF16).reshape(nch, SSM_CHUNK, g_n, n_in).transpose(2, 0, 1, 3).reshape(g_n, nch, -1)
        y_r = _ssm(u_r, ssm_a_re[l], ssm_a_im[l], ssm_log_dt[l], ssm_b_re[l], ssm_b_im[l], ssm_c_re[l],
                   ssm_c_im[l], ssm_d[l])
        y_d = y_r.reshape(g_n, nch, SSM_CHUNK, n_in).transpose(1, 2, 0, 3).reshape(s, GROUP_WIDTH)
        xf = _mix_out(xf, o_a, o_b, o_c, y_d, ssm_glu_w[l].astype(BF16), ssm_glu_b[l], mix_out_g[l],
                      w_out[l].astype(BF16))
        (kv_mem,) = _norm_proj(memf, mem_ln_g[l], jnp.concatenate([xa_wk[l], xa_wv[l]], axis=-1).astype(BF16),
                               (2 * XA_HEADS * XA_HEAD_DIM,), memf.shape[0])
        xf = _xattn(xf, ln_cross_g[l], xa_wq[l].astype(BF16), xa_qn_g[l], kv_mem, xa_kn_g[l],
                    xa_wo[l].astype(BF16))
        xf = _moe_layer(xf, ln_moe_g[l], moe_w_group[l], moe_b_group[l], moe_w_expert[l], moe_b_expert[l],
                        moe_w_gate, moe_w_up, moe_w_down, l)
    return xf.reshape(b, s, d)
```

```python
import functools
import math

import jax
import jax.numpy as jnp
import numpy as np
from jax import lax
from jax.experimental import pallas as pl
from jax.experimental.pallas import tpu as pltpu

F32 = jnp.float32
BF16 = jnp.bfloat16

D_MODEL = 2048
DEPTH = 4
GROUP_WIDTH = 512
SWA_HEADS = 8
SWA_KV_HEADS = 2
SWA_HEAD_DIM = 64
SWA_BLOCK = 128
CONV_WIDTH = 31
MLA_HEADS = 4
MLA_NOPE = 128
MLA_ROPE = 64
MLA_V = 128
MLA_QK_PAD = 256
ROPE_THETA = 10000.0
SSM_GROUPS = 32
SSM_GROUP_IN = 16
SSM_STATE = 64
SSM_CHUNK = 64
REL_BUCKETS = 32
REL_MAX_DIST = 128
XA_HEADS = 4
XA_HEAD_DIM = 128
N_EXPERT_GROUPS = 4
EXPERTS_PER_GROUP = 8
N_EXPERTS = 32
EXPERT_FF = 512
EXPERT_BLOCK = 256
NORM_EPS = 1e-6
NEG_INF = -1e30
LOG2E = 1.4426950408889634

PA_W = 768
PB_W = 1024
PC_W = 896
PD_W = 512
P_SEGS = (PA_W, PB_W, PC_W, PD_W)

V7X_VMEM_LIMIT = 48 * 1024 * 1024

NT_DIMS = (((1,), (1,)), ((), ()))


def _cparams(sem, vmem=V7X_VMEM_LIMIT):
    return pltpu.CompilerParams(dimension_semantics=sem, vmem_limit_bytes=vmem)


def _const_spec(shape):
    nd = len(shape)
    return pl.BlockSpec(shape, lambda *_: (0,) * nd, pipeline_mode=pl.Buffered(1))


def _rms(x, g):
    return x * lax.rsqrt(jnp.mean(x * x, axis=-1, keepdims=True) + NORM_EPS) * g


def _dot(a, b):
    return jnp.dot(a, b, preferred_element_type=F32)


def _dot_nt(a, b):
    return lax.dot_general(a, b, NT_DIMS, preferred_element_type=F32)


def _head_rms_mxu(x, gains, width):
    n = x.shape[-1]
    same = (lax.broadcasted_iota(jnp.int32, (n, n), 0) // width == lax.broadcasted_iota(jnp.int32, (n, n), 1) // width)
    ones = same.astype(BF16)
    x2 = x * x
    hi = x2.astype(BF16)
    lo = (x2 - hi.astype(F32)).astype(BF16)
    ss = _dot(hi, ones) + _dot(lo, ones)
    return x * lax.rsqrt(ss * (1.0 / width) + NORM_EPS) * gains


def _norm_proj_kernel(x_ref, g_ref, w_ref, *refs, segs, n_t):
    h = _rms(x_ref[...], g_ref[...]).astype(BF16)
    o_refs = refs[1:] if n_t else refs
    off = 0
    for o_ref, n in zip(o_refs, segs):
        o_ref[...] = _dot(h, w_ref[:, off:off + n])
        off += n
    if n_t:
        o_refs[len(segs)][...] = _dot_nt(refs[0][...], h).astype(BF16)


def _norm_proj(x, g, w, segs, tm, wt=None):
    s, d = x.shape
    n = sum(segs)
    n_t = 0 if wt is None else wt.shape[0]
    in_specs = [pl.BlockSpec((tm, d), lambda i: (i, 0)), _const_spec((1, d)), _const_spec((d, n))]
    out_shape = [jax.ShapeDtypeStruct((s, n_i), F32) for n_i in segs]
    out_specs = [pl.BlockSpec((tm, n_i), lambda i: (i, 0)) for n_i in segs]
    args = [x, g.reshape(1, d), w]
    if n_t:
        in_specs.append(_const_spec((n_t, d)))
        out_shape.append(jax.ShapeDtypeStruct((n_t, s), BF16))
        out_specs.append(pl.BlockSpec((n_t, tm), lambda i: (0, i)))
        args.append(wt)
    return pl.pallas_call(
        functools.partial(_norm_proj_kernel, segs=segs, n_t=n_t),
        out_shape=out_shape,
        grid=(s // tm,),
        in_specs=in_specs,
        out_specs=out_specs,
        compiler_params=_cparams(("parallel",)),
        name="norm_proj",
    )(*args)


def _t5_buckets():
    qi = np.arange(SWA_BLOCK)[:, None]
    kj = np.arange(2 * SWA_BLOCK)[None, :]
    dist = SWA_BLOCK + qi - kj
    valid = (dist >= 0) & (dist < SWA_BLOCK)
    max_exact = REL_BUCKETS // 2
    d0 = np.maximum(dist, 0)
    dl = np.maximum(dist, 1).astype(np.float32)
    large = max_exact + (np.log(dl / max_exact) / np.log(REL_MAX_DIST / max_exact)
                         * (REL_BUCKETS - max_exact)).astype(np.int32)
    large = np.minimum(large, REL_BUCKETS - 1)
    bucket = np.where(d0 < max_exact, d0, large).astype(np.int32)
    return bucket, valid.astype(np.int32)


def _bias_kernel(rb_ref, bucket_ref, valid_ref, o_ref):
    b = bucket_ref[...]
    ok = valid_ref[...] > 0
    for h in range(SWA_HEADS):
        acc = jnp.zeros(b.shape, F32)
        for k in range(REL_BUCKETS):
            acc = jnp.where(b == k, rb_ref[k, h], acc)
        o_ref[h] = jnp.where(ok, acc, NEG_INF)


def _bias_table(rel_bias):
    bucket, valid = _t5_buckets()
    tab = pl.pallas_call(
        _bias_kernel,
        out_shape=jax.ShapeDtypeStruct((SWA_HEADS, SWA_BLOCK, 2 * SWA_BLOCK), F32),
        in_specs=[pl.BlockSpec(memory_space=pltpu.SMEM),
                  pl.BlockSpec(memory_space=pltpu.VMEM), pl.BlockSpec(memory_space=pltpu.VMEM)],
        out_specs=pl.BlockSpec(memory_space=pltpu.VMEM),
        name="t5_bias_table",
    )(rel_bias, jnp.asarray(bucket), jnp.asarray(valid))
    grp = SWA_HEADS // SWA_KV_HEADS
    tab = tab.reshape(SWA_KV_HEADS, grp, SWA_BLOCK, 2 * SWA_BLOCK).transpose(0, 3, 1, 2)
    return tab.reshape(SWA_KV_HEADS, 2 * SWA_BLOCK, grp * SWA_BLOCK)


SWA_QB = 4


def _swa_kernel(sink_ref, cur_ref, prev_ref, vt_cur_ref, vt_prev_ref, qg_ref, kg_ref, bias_ref, o_ref, ot_scr):
    i = pl.program_id(0)
    blk, dh = SWA_BLOCK, SWA_HEAD_DIM
    grp = SWA_HEADS // SWA_KV_HEADS
    scale = dh ** -0.5

    qn = _head_rms_mxu(cur_ref[:, 0:512], qg_ref[...], dh) * scale
    k_all = jnp.concatenate([prev_ref[:, 512:640], cur_ref[:, 512:640]], axis=0)
    kn = _head_rms_mxu(k_all, kg_ref[...], dh)
    vt_all = jnp.concatenate([vt_prev_ref[...], vt_cur_ref[...]], axis=1)
    lane_k = lax.broadcasted_iota(jnp.int32, kn.shape, 1)
    lane_q = lax.broadcasted_iota(jnp.int32, (blk, 128), 1)
    key_row = lax.broadcasted_iota(jnp.int32, (2 * blk, grp * blk), 0)
    for kh in range(SWA_KV_HEADS):
        k_one = jnp.where(lane_k // dh == kh, kn, 0.0)
        k_dup = (k_one + pltpu.roll(k_one, dh, 1)).astype(BF16)
        sink = jnp.concatenate([jnp.full((1, blk), sink_ref[kh * grp + g], F32) for g in range(grp)], axis=1)
        for j in range(SWA_QB):
            rows = slice(j * blk, (j + 1) * blk)
            qs = []
            for g in range(grp):
                h = kh * grp + g
                q_pair = qn[rows, (h // 2) * 128:(h // 2 + 1) * 128]
                qs.append(jnp.where(lane_q // dh == h % 2, q_pair, 0.0))
            q_st = jnp.concatenate(qs, axis=0).astype(BF16)
            st = _dot_nt(k_dup[j * blk:(j + 2) * blk], q_st) + bias_ref[kh]
            if j == 0:
                st = jnp.where(jnp.logical_and(i == 0, key_row < blk), NEG_INF, st)
            m = jnp.maximum(jnp.max(st, axis=0, keepdims=True), sink)
            p = jnp.exp(st - m)
            denom = jnp.sum(p, axis=0, keepdims=True) + jnp.exp(sink - m)
            ot = _dot(vt_all[kh * dh:(kh + 1) * dh, j * blk:(j + 2) * blk], p.astype(BF16)) / denom
            for g in range(grp):
                h = kh * grp + g
                ot_scr[h * dh:(h + 1) * dh, rows] = ot[:, g * blk:(g + 1) * blk]
    o_ref[...] = ot_scr[...].T


def _swa(pa, vt, qn_g, kn_g, sinks, bias_t):
    s = pa.shape[0]
    tq = SWA_QB * SWA_BLOCK
    grp = SWA_HEADS // SWA_KV_HEADS
    prev = lambda i, sk: (jnp.maximum(i * SWA_QB - 1, 0), 0)
    return pl.pallas_call(
        _swa_kernel,
        out_shape=jax.ShapeDtypeStruct((s, GROUP_WIDTH), F32),
        grid_spec=pltpu.PrefetchScalarGridSpec(
            num_scalar_prefetch=1,
            grid=(s // tq,),
            in_specs=[pl.BlockSpec((tq, PA_W), lambda i, sk: (i, 0)),
                      pl.BlockSpec((SWA_BLOCK, PA_W), prev),
                      pl.BlockSpec((128, tq), lambda i, sk: (0, i)),
                      pl.BlockSpec((128, SWA_BLOCK), lambda i, sk: (0, jnp.maximum(i * SWA_QB - 1, 0))),
                      _const_spec((1, GROUP_WIDTH)), _const_spec((1, 128)),
                      _const_spec((SWA_KV_HEADS, 2 * SWA_BLOCK, grp * SWA_BLOCK))],
            out_specs=pl.BlockSpec((tq, GROUP_WIDTH), lambda i, sk: (i, 0)),
            scratch_shapes=[pltpu.VMEM((GROUP_WIDTH, tq), F32)],
        ),
        compiler_params=_cparams(("parallel",)),
        name="swa_attention",
    )(sinks, pa, pa, vt, vt, jnp.tile(qn_g, SWA_HEADS).reshape(1, -1), jnp.tile(kn_g, SWA_KV_HEADS).reshape(1, -1),
      bias_t)


CONV_TILE = 256
CONV_HALO = 32


def _conv_kernel(cur_ref, prev_ref, dww_ref, dwb_ref, lng_ref, lnb_ref, pww_ref, pwb_ref, o_ref, z_scr, zph_scr):
    i = pl.program_id(0)
    t, c = CONV_TILE, GROUP_WIDTH
    z_scr[CONV_HALO:CONV_HALO + t, :] = cur_ref[:, 0:c] * jax.nn.sigmoid(cur_ref[:, c:2 * c])
    zp = prev_ref[:, 0:c] * jax.nn.sigmoid(prev_ref[:, c:2 * c])
    z_scr[0:CONV_HALO, :] = jnp.where(i > 0, zp, 0.0)
    base = CONV_HALO - (CONV_WIDTH - 1)
    span = t + CONV_HALO - 8
    for ph in range(1, 8):
        zph_scr[ph - 1] = z_scr[ph:ph + span, :]
    acc = jnp.zeros((t, c), F32)
    for k in range(CONV_WIDTH):
        ph, off = (base + k) % 8, (base + k) // 8 * 8
        tap = z_scr[off:off + t, :] if ph == 0 else zph_scr[ph - 1, off:off + t, :]
        acc = acc + dww_ref[k:k + 1, :] * tap
    acc = acc + dwb_ref[...]
    xc = acc - jnp.mean(acc, axis=-1, keepdims=True)
    y = xc * lax.rsqrt(jnp.mean(xc * xc, axis=-1, keepdims=True) + NORM_EPS) * lng_ref[...] + lnb_ref[...]
    y = y * jax.nn.sigmoid(y)
    o_ref[...] = _dot(y.astype(BF16), pww_ref[...]) + pwb_ref[...]


def _conv(pb, dw_w, dw_b, ln_g, ln_b, pw_w, pw_b):
    s = pb.shape[0]
    c = GROUP_WIDTH
    r = CONV_TILE // CONV_HALO
    row = lambda v: v.reshape(1, c)
    return pl.pallas_call(
        _conv_kernel,
        out_shape=jax.ShapeDtypeStruct((s, c), F32),
        grid=(s // CONV_TILE,),
        in_specs=[pl.BlockSpec((CONV_TILE, PB_W), lambda i: (i, 0)),
                  pl.BlockSpec((CONV_HALO, PB_W), lambda i: (jnp.maximum(i * r - 1, 0), 0)),
                  _const_spec((CONV_WIDTH, c)), _const_spec((1, c)), _const_spec((1, c)), _const_spec((1, c)),
                  _const_spec((c, c)), _const_spec((1, c))],
        out_specs=pl.BlockSpec((CONV_TILE, c), lambda i: (i, 0)),
        scratch_shapes=[pltpu.VMEM((CONV_TILE + CONV_HALO, c), F32),
                        pltpu.VMEM((7, CONV_TILE + CONV_HALO - 8, c), F32)],
        compiler_params=_cparams(("parallel",)),
        name="conformer_conv",
    )(pb, pb, dw_w, row(dw_b), row(ln_g), row(ln_b), pw_w, row(pw_b))


MLA_TILE = 256


def _rot_half64(x):
    n = x.shape[-1]
    ax = x.ndim - 1
    lane = lax.broadcasted_iota(jnp.int32, x.shape, ax)
    return jnp.where((lane % 64) < 32, -pltpu.roll(x, n - 32, ax), pltpu.roll(x, 32, ax))


def _seg_rms64(x):
    n = x.shape[-1]
    lane = lax.broadcasted_iota(jnp.int32, x.shape, x.ndim - 1)
    x2 = x * x
    r = jnp.zeros(x.shape, F32)
    for h in range(n // 64):
        msk = jnp.logical_and(lane >= h * 64, lane < (h + 1) * 64)
        sm = jnp.sum(jnp.where(msk, x2, 0.0), axis=-1, keepdims=True)
        r = jnp.where(msk, lax.rsqrt(sm * (1.0 / 64) + NORM_EPS), r)
    return x * r


def _mla_proj_kernel(pc_ref, cos_ref, sin_ref, qag_ref, kvag_ref, wq_ref, wkv_ref,
                     qng_ref, qrg_ref, kng_ref, krg_ref, q_ref, k_ref, v_ref):
    h_n, dn = MLA_HEADS, MLA_NOPE
    scale = (MLA_NOPE + MLA_ROPE) ** -0.5 * LOG2E
    cos4 = cos_ref[...]
    sin4 = sin_ref[...]
    qn = _rms(pc_ref[:, 0:512], qag_ref[...]).astype(BF16)
    q = _dot(qn, wq_ref[...])
    kvn = _rms(pc_ref[:, 512:768], kvag_ref[...]).astype(BF16)
    kv = _dot(kvn, wkv_ref[...])
    qr = _seg_rms64(q[:, h_n * dn:]) * qrg_ref[...]
    qr = (qr * cos4 + _rot_half64(qr) * sin4) * scale
    kr = pc_ref[:, 768:896]
    kr = kr * lax.rsqrt(jnp.sum(kr * kr, axis=-1, keepdims=True) * (1.0 / MLA_ROPE) + NORM_EPS) * krg_ref[...]
    kr = (kr * cos4[:, 0:128] + _rot_half64(kr) * sin4[:, 0:128]).astype(BF16)
    lane = lax.broadcasted_iota(jnp.int32, (q.shape[0], 128), 1)
    for h in range(h_n):
        qnope = _rms(q[:, h * dn:(h + 1) * dn], qng_ref[...]) * scale
        blk = qr[:, (h // 2) * 128:(h // 2 + 1) * 128]
        if h % 2 == 1:
            blk = pltpu.roll(blk, 64, 1)
        blk = jnp.where(lane < 64, blk, 0.0)
        q_ref[h] = jnp.concatenate([qnope, blk], axis=-1).astype(BF16)
        knope = _rms(kv[:, h * dn:(h + 1) * dn], kng_ref[...]).astype(BF16)
        k_ref[h] = jnp.concatenate([knope, kr], axis=-1)
        v_ref[h] = kv[:, (h_n + h) * dn:(h_n + h + 1) * dn].astype(BF16)


def _mla_proj(pc, cos4, sin4, qa_g, kva_g, wq, wkv, qn_nope_g, qn_rope_g4, kn_nope_g, kn_rope_g_pad):
    s = pc.shape[0]
    t = MLA_TILE
    h = MLA_HEADS
    row = lambda v: v.reshape(1, -1)
    return pl.pallas_call(
        _mla_proj_kernel,
        out_shape=[jax.ShapeDtypeStruct((h, s, MLA_QK_PAD), BF16),
                   jax.ShapeDtypeStruct((h, s, MLA_QK_PAD), BF16),
                   jax.ShapeDtypeStruct((h, s, MLA_V), BF16)],
        grid=(s // t,),
        in_specs=[pl.BlockSpec((t, PC_W), lambda i: (i, 0)),
                  pl.BlockSpec((t, 256), lambda i: (i, 0)), pl.BlockSpec((t, 256), lambda i: (i, 0)),
                  _const_spec((1, 512)), _const_spec((1, 256)),
                  _const_spec((512, 768)), _const_spec((256, 1024)),
                  _const_spec((1, 128)), _const_spec((1, 256)), _const_spec((1, 128)), _const_spec((1, 128))],
        out_specs=[pl.BlockSpec((h, t, MLA_QK_PAD), lambda i: (0, i, 0)),
                   pl.BlockSpec((h, t, MLA_QK_PAD), lambda i: (0, i, 0)),
                   pl.BlockSpec((h, t, MLA_V), lambda i: (0, i, 0))],
        compiler_params=_cparams(("parallel",)),
        name="mla_proj",
    )(pc, cos4, sin4, row(qa_g), row(kva_g), wq, wkv, row(qn_nope_g), row(qn_rope_g4), row(kn_nope_g),
      row(kn_rope_g_pad))


FLASH_BQ = 2048
FLASH_BK = 512
FLASH_SUM_ROWS = 16


def _flash_kernel(q_ref, k_ref, vt_ref, o_ref, sa_scr, sb_scr, m_scr, acc_scr):
    bq, bk = FLASH_BQ, FLASH_BK
    per_q = bq // bk
    assert per_q % 2 == 0
    qi = pl.program_id(1)
    m_scr[...] = jnp.full(m_scr.shape, NEG_INF, F32)
    acc_scr[...] = jnp.zeros(acc_scr.shape, F32)
    ones_rows = jnp.ones((FLASH_SUM_ROWS, bk), BF16)

    def scores(j, s_scr, c0=0):
        k = k_ref[pl.ds(pl.multiple_of(j * bk, bk), bk), :]
        s_scr[:, c0:] = _dot_nt(k, q_ref[c0:, :])

    def update(j, s_scr, masked, c0=0):
        st = s_scr[:, c0:]
        if masked:
            kpos = j * bk + lax.broadcasted_iota(jnp.int32, st.shape, 0)
            qpos = qi * bq + c0 + lax.broadcasted_iota(jnp.int32, st.shape, 1)
            st = jnp.where(kpos <= qpos, st, NEG_INF)
        m_prev = m_scr[:, c0:]
        m_new = jnp.maximum(m_prev, jnp.max(st, axis=0, keepdims=True))
        alpha = jnp.exp2(m_prev - m_new)
        p = jnp.exp2(st - m_new)
        vt_ext = jnp.concatenate([vt_ref[j], ones_rows], axis=0)
        acc_scr[:, c0:] = alpha * acc_scr[:, c0:] + _dot(vt_ext, p.astype(BF16))
        m_scr[:, c0:] = m_new

    scores(0, sa_scr)

    def body(i, carry):
        scores(2 * i + 1, sb_scr)
        update(2 * i, sa_scr, False)
        scores(2 * i + 2, sa_scr)
        update(2 * i + 1, sb_scr, False)
        return carry

    lax.fori_loop(0, qi * (per_q // 2), body, 0)
    first = qi * per_q
    bufs = (sa_scr, sb_scr)
    for c in range(per_q):
        if c + 1 < per_q:
            scores(first + c + 1, bufs[(c + 1) % 2], (c + 1) * bk)
        update(first + c, bufs[c % 2], True, c * bk)
    o_ref[...] = (acc_scr[0:MLA_V, :] / acc_scr[MLA_V:MLA_V + 1, :]).T


def _flash(q, k, vt):
    h, s, _ = q.shape
    nkb = s // FLASH_BK
    return pl.pallas_call(
        _flash_kernel,
        out_shape=jax.ShapeDtypeStruct((s, h * MLA_V), F32),
        grid=(h, s // FLASH_BQ),
        in_specs=[pl.BlockSpec((None, FLASH_BQ, MLA_QK_PAD), lambda hh, i: (hh, i, 0)),
                  pl.BlockSpec((None, s, MLA_QK_PAD), lambda hh, i: (hh, 0, 0)),
                  pl.BlockSpec((None, nkb, MLA_V, FLASH_BK), lambda hh, i: (hh, 0, 0, 0))],
        out_specs=pl.BlockSpec((FLASH_BQ, MLA_V), lambda hh, i: (i, hh)),
        scratch_shapes=[pltpu.VMEM((FLASH_BK, FLASH_BQ), F32), pltpu.VMEM((FLASH_BK, FLASH_BQ), F32),
                        pltpu.VMEM((1, FLASH_BQ), F32), pltpu.VMEM((MLA_V + FLASH_SUM_ROWS, FLASH_BQ), F32)],
        compiler_params=_cparams(("parallel", "arbitrary"), vmem=56 * 1024 * 1024),
        name="mla_flash",
    )(q, k, vt)


def _split3_dot(a, sel):
    a1 = a.astype(BF16)
    r1 = a - a1.astype(F32)
    a2 = r1.astype(BF16)
    a3 = (r1 - a2.astype(F32)).astype(BF16)
    return _dot(a1, sel) + _dot(a2, sel) + _dot(a3, sel)


def _hi_dot(a, b):
    return lax.dot_general(a, b, (((1,), (0,)), ((), ())), precision=lax.Precision.HIGHEST,
                           preferred_element_type=F32)


def _ssm_kernel(u_ref, arc_ref, aic_ref, arr_ref, air_ref, ldt_ref, bre_ref, bim_ref, btre_ref, btim_ref,
                ctre_ref, ctim_ref, dcol_ref, y_ref, toep_scr, *, nch):
    tc, n_in, n_st = SSM_CHUNK, SSM_GROUP_IN, SSM_STATE
    w = tc * n_in
    dt = jnp.exp(ldt_ref[...])
    ar_c, ai_c = arc_ref[...], aic_ref[...]
    ar_r, ai_r = arr_ref[...], air_ref[...]

    def lam_pow(ar, ai, e):
        mag = jnp.exp(ar * dt * e)
        ang = ai * dt * e
        return mag * jnp.cos(ang), mag * jnp.sin(ang)

    def zoh_coef(ar, ai):
        lr, li = lam_pow(ar, ai, 1.0)
        x, y = lr - 1.0, li
        den = ar * ar + ai * ai
        return (x * ar + y * ai) / den, (y * ar - x * ai) / den

    lane_w = lax.broadcasted_iota(jnp.int32, (tc, w), 1)
    sel_tau = (lane_w // n_in == lax.broadcasted_iota(jnp.int32, (tc, w), 0)).astype(BF16)
    lane_o = lax.broadcasted_iota(jnp.int32, (n_in, w), 1)
    sel_o = (lane_o % n_in == lax.broadcasted_iota(jnp.int32, (n_in, w), 0)).astype(BF16)

    tau = lax.broadcasted_iota(jnp.int32, (n_st, tc), 1).astype(F32)
    lre, lim = lam_pow(ar_c, ai_c, tau)
    lre_x, lim_x = _split3_dot(lre, sel_tau), _split3_dot(lim, sel_tau)
    rre, rim = lam_pow(ar_c, ai_c, (tc - 1.0) - tau)
    rre_x, rim_x = _split3_dot(rre, sel_tau), _split3_dot(rim, sel_tau)
    ctre_x, ctim_x = _split3_dot(ctre_ref[...], sel_o), _split3_dot(ctim_ref[...], sel_o)

    e_re = lre_x * ctre_x - lim_x * ctim_x
    e_im = lre_x * ctim_x + lim_x * ctre_x

    cr_c, ci_c = zoh_coef(ar_c, ai_c)
    bb_re = cr_c * bre_ref[...] - ci_c * bim_ref[...]
    bb_im = cr_c * bim_ref[...] + ci_c * bre_ref[...]
    bb_re_x, bb_im_x = _split3_dot(bb_re, sel_o), _split3_dot(bb_im, sel_o)
    f_re = (rre_x * bb_re_x - rim_x * bb_im_x).astype(BF16)
    f_im = (rre_x * bb_im_x + rim_x * bb_re_x).astype(BF16)

    cr_r, ci_r = zoh_coef(ar_r, ai_r)
    bbt_re = cr_r * btre_ref[...] - ci_r * btim_ref[...]
    bbt_im = cr_r * btim_ref[...] + ci_r * btre_ref[...]
    g_row = _hi_dot(bbt_re, e_re) - _hi_dot(bbt_im, e_im)
    lane_g = lax.broadcasted_iota(jnp.int32, (n_in, w), 1)
    row_g = lax.broadcasted_iota(jnp.int32, (n_in, w), 0)
    g_row = g_row + jnp.where(lane_g == row_g, dcol_ref[...], 0.0)
    for s in range(tc):
        rolled = g_row if s == 0 else pltpu.roll(g_row, n_in * s, 1)
        toep_scr[s * n_in:(s + 1) * n_in, :] = jnp.where(lane_g >= n_in * s, rolled, 0.0).astype(BF16)

    u = u_ref[...]
    s_re = _dot_nt(u, f_re)
    s_im = _dot_nt(u, f_im)
    row = lax.broadcasted_iota(jnp.int32, (nch, n_st), 0)
    sh = 1
    while sh < nch:
        pr, pi = lam_pow(ar_r, ai_r, float(tc * sh))
        re_s = jnp.where(row >= sh, pltpu.roll(s_re, sh, 0), 0.0)
        im_s = jnp.where(row >= sh, pltpu.roll(s_im, sh, 0), 0.0)
        s_re, s_im = s_re + pr * re_s - pi * im_s, s_im + pr * im_s + pi * re_s
        sh *= 2
    p_re = jnp.where(row >= 1, pltpu.roll(s_re, 1, 0), 0.0)
    p_im = jnp.where(row >= 1, pltpu.roll(s_im, 1, 0), 0.0)
    l1r, l1i = lam_pow(ar_c, ai_c, 1.0)
    h_re = (l1r * e_re - l1i * e_im).astype(BF16)
    h_im = (l1r * e_im + l1i * e_re).astype(BF16)
    y = _dot(u, toep_scr[...]) + _dot(p_re.astype(BF16), h_re) - _dot(p_im.astype(BF16), h_im)
    y_ref[...] = y


def _ssm(u_r, a_re, a_im, log_dt, b_re, b_im, c_re, c_im, d_skip):
    g, nch, w = u_r.shape
    n_st, n_in = SSM_STATE, SSM_GROUP_IN
    col = lambda v: v.reshape(g, n_st, 1)
    rw = lambda v: v.reshape(g, 1, n_st)
    spec = lambda *shape: pl.BlockSpec((None,) + shape, lambda i: (i,) + (0,) * len(shape))
    return pl.pallas_call(
        functools.partial(_ssm_kernel, nch=nch),
        out_shape=jax.ShapeDtypeStruct((g, nch, w), F32),
        grid=(g,),
        in_specs=[spec(nch, w), spec(n_st, 1), spec(n_st, 1), spec(1, n_st), spec(1, n_st), spec(1, 1),
                  spec(n_st, n_in), spec(n_st, n_in), spec(n_in, n_st), spec(n_in, n_st),
                  spec(n_st, n_in), spec(n_st, n_in), spec(n_in, 1)],
        out_specs=spec(nch, w),
        scratch_shapes=[pltpu.VMEM((w, w), BF16)],
        compiler_params=_cparams(("parallel",)),
        name="s5_ssm",
    )(u_r, col(a_re), col(a_im), rw(a_re), rw(a_im), log_dt.reshape(g, 1, 1), b_re, b_im,
      jnp.swapaxes(b_re, 1, 2), jnp.swapaxes(b_im, 1, 2), jnp.swapaxes(c_re, 1, 2), jnp.swapaxes(c_im, 1, 2),
      d_skip.reshape(g, n_in, 1))


OUT_TILE = 256


def _mix_out_kernel(x_ref, oa_ref, ob_ref, oc_ref, yd_ref, gw_ref, gb_ref, mg_ref, wo_ref, o_ref):
    c = GROUP_WIDTH
    y = jax.nn.gelu(yd_ref[...])
    od = y * jax.nn.sigmoid(_dot(y.astype(BF16), gw_ref[...]) + gb_ref[...])
    acc = x_ref[...]
    for m, o in enumerate((oa_ref[...], ob_ref[...], oc_ref[...], od)):
        n = _rms(o, mg_ref[:, m * c:(m + 1) * c]).astype(BF16)
        acc = acc + _dot(n, wo_ref[m * c:(m + 1) * c, :])
    o_ref[...] = acc


def _mix_out(x, oa, ob, oc, yd, glu_w, glu_b, mix_g, w_out):
    s, d = x.shape
    c = GROUP_WIDTH
    t = OUT_TILE
    tile = lambda n: pl.BlockSpec((t, n), lambda i: (i, 0))
    return pl.pallas_call(
        _mix_out_kernel,
        out_shape=jax.ShapeDtypeStruct((s, d), F32),
        grid=(s // t,),
        in_specs=[tile(d), tile(c), tile(c), tile(c), tile(c),
                  _const_spec((c, c)), _const_spec((1, c)), _const_spec((1, d)), _const_spec((d, d))],
        out_specs=tile(d),
        compiler_params=_cparams(("parallel",)),
        name="mix_out",
    )(x, oa, ob, oc, yd, glu_w, glu_b.reshape(1, c), mix_g.reshape(1, d), w_out)


XA_TILE = 256


def _xattn_kernel(x_ref, lg_ref, wq_ref, qg_ref, k_ref, vt_ref, kg_ref, wo_ref, o_ref):
    dh = XA_HEAD_DIM
    x = x_ref[...]
    q = _dot(_rms(x, lg_ref[...]).astype(BF16), wq_ref[...])
    qn = (_head_rms_mxu(q, qg_ref[...], dh) * dh ** -0.5).astype(BF16)
    kn = _head_rms_mxu(k_ref[...], kg_ref[...], dh).astype(BF16)
    outs = []
    for h in range(XA_HEADS):
        cols = slice(h * dh, (h + 1) * dh)
        st = _dot_nt(kn[:, cols], qn[:, cols])
        p = jnp.exp(st - jnp.max(st, axis=0, keepdims=True))
        outs.append(_dot(vt_ref[cols, :], p.astype(BF16)) / jnp.sum(p, axis=0, keepdims=True))
    o = jnp.concatenate(outs, axis=0).T
    o_ref[...] = x + _dot(o.astype(BF16), wo_ref[...])


def _xattn(x, ln_g, wq, qn_g, k_mem, vt_mem, kn_g, wo):
    s, d = x.shape
    m, hw = k_mem.shape
    t = XA_TILE
    return pl.pallas_call(
        _xattn_kernel,
        out_shape=jax.ShapeDtypeStruct((s, d), F32),
        grid=(s // t,),
        in_specs=[pl.BlockSpec((t, d), lambda i: (i, 0)), _const_spec((1, d)), _const_spec((d, hw)),
                  _const_spec((1, hw)), _const_spec((m, hw)), _const_spec((hw, m)), _const_spec((1, hw)),
                  _const_spec((hw, d))],
        out_specs=pl.BlockSpec((t, d), lambda i: (i, 0)),
        compiler_params=_cparams(("parallel",)),
        name="mem_xattn",
    )(x, ln_g.reshape(1, d), wq, jnp.tile(qn_g, XA_HEADS).reshape(1, -1), k_mem, vt_mem,
      jnp.tile(kn_g, XA_HEADS).reshape(1, -1), wo)


ROUTER_TILE = 256
ROUTER_LANES = 128


def _router_kernel(x_ref, lg_ref, whi_ref, wlo_ref, b_ref, h_ref, r_ref, cnt_ref, carry_scr):
    i = pl.program_id(0)
    t = ROUTER_TILE
    ng, epg = N_EXPERT_GROUPS, EXPERTS_PER_GROUP

    @pl.when(i == 0)
    def _():
        carry_scr[...] = jnp.zeros(carry_scr.shape, F32)

    h = _rms(x_ref[...], lg_ref[...])
    h_hi = h.astype(BF16)
    h_lo = (h - h_hi.astype(F32)).astype(BF16)
    h_ref[...] = _pack_bf16_pair(h[:, 0:D_MODEL // 2], h[:, D_MODEL // 2:])
    logits = _dot(h_hi, whi_ref[...]) + _dot(h_hi, wlo_ref[...]) + _dot(h_lo, whi_ref[...]) + b_ref[...]
    lane = lax.broadcasted_iota(jnp.int32, (t, ROUTER_LANES), 1)
    big = jnp.int32(ROUTER_LANES)

    def first_max(vals, mask):
        v = jnp.where(mask, vals, -jnp.inf)
        mx = jnp.max(v, axis=-1, keepdims=True)
        idx = jnp.min(jnp.where(jnp.logical_and(mask, v == mx), lane, big), axis=-1, keepdims=True)
        return mx, idx

    gmask = lane < ng
    gmax, gidx = first_max(logits, gmask)
    g_w = 1.0 / jnp.sum(jnp.where(gmask, jnp.exp(logits - gmax), 0.0), axis=-1, keepdims=True)
    emask = jnp.logical_and(lane >= ng + gidx * epg, lane < ng + (gidx + 1) * epg)
    v1, i1 = first_max(logits, emask)
    v2, i2 = first_max(logits, jnp.logical_and(emask, lane != i1))
    e2 = jnp.exp(v2 - v1)
    gate1 = g_w / (1.0 + e2)
    gate2 = g_w * e2 / (1.0 + e2)

    oh1 = lane == i1
    oh2 = lane == i2
    both = (oh1.astype(F32) + oh2.astype(F32))
    tri = (lax.broadcasted_iota(jnp.int32, (t, t), 1) < lax.broadcasted_iota(jnp.int32, (t, t), 0)).astype(BF16)
    before = _dot(tri, both.astype(BF16)) + carry_scr[...]
    rank1 = jnp.sum(jnp.where(oh1, before, 0.0), axis=-1, keepdims=True)
    rank2 = jnp.sum(jnp.where(oh2, before, 0.0), axis=-1, keepdims=True)
    carry_scr[...] = carry_scr[...] + jnp.sum(both, axis=0, keepdims=True)
    cnt_ref[...] = jnp.broadcast_to(carry_scr[...], cnt_ref.shape)

    out = jnp.zeros((t, ROUTER_LANES), F32)
    for k, val in enumerate(((i1 - ng).astype(F32), (i2 - ng).astype(F32), gate1, gate2, rank1, rank2)):
        out = jnp.where(lane == k, val, out)
    r_ref[...] = out


def _router(x, ln_g, w_hi, w_lo, bias):
    s, d = x.shape
    t = ROUTER_TILE
    return pl.pallas_call(
        _router_kernel,
        out_shape=[jax.ShapeDtypeStruct((s, d // 2), jnp.uint32), jax.ShapeDtypeStruct((s, ROUTER_LANES), F32),
                   jax.ShapeDtypeStruct((8, ROUTER_LANES), F32)],
        grid=(s // t,),
        in_specs=[pl.BlockSpec((t, d), lambda i: (i, 0)), _const_spec((1, d)),
                  _const_spec((d, ROUTER_LANES)), _const_spec((d, ROUTER_LANES)), _const_spec((1, ROUTER_LANES))],
        out_specs=[pl.BlockSpec((t, d // 2), lambda i: (i, 0)), pl.BlockSpec((t, ROUTER_LANES), lambda i: (i, 0)),
                   pl.BlockSpec((8, ROUTER_LANES), lambda i: (0, 0))],
        scratch_shapes=[pltpu.VMEM((1, ROUTER_LANES), F32)],
        compiler_params=_cparams(("arbitrary",)),
        name="moe_router",
    )(x, ln_g.reshape(1, d), w_hi, w_lo, bias)


def _ffn_kernel(be_ref, nu_ref, x_ref, wg_ref, wu_ref, wd_ref, o_ref, wg_scr, wu_scr, wd_scr):
    b = pl.program_id(0)
    half = D_MODEL // 2
    prev = be_ref[jnp.maximum(b - 1, 0)]

    @pl.when(jnp.logical_or(b == 0, be_ref[b] != prev))
    def _():
        wg_scr[...] = wg_ref[...].astype(BF16)
        wu_scr[...] = wu_ref[...].astype(BF16)
        wd_scr[...] = wd_ref[...].astype(BF16)

    @pl.when(b < nu_ref[0])
    def _():
        xa, xb = _unpack_bf16_pair(x_ref[...])
        xa, xb = xa.astype(BF16), xb.astype(BF16)
        g = _dot(xa, wg_scr[0:half, :]) + _dot(xb, wg_scr[half:, :])
        u = _dot(xa, wu_scr[0:half, :]) + _dot(xb, wu_scr[half:, :])
        hid = ((g * jax.nn.sigmoid(g)) * u).astype(BF16)
        o_ref[...] = _pack_bf16_pair(_dot(hid, wd_scr[:, 0:half]), _dot(hid, wd_scr[:, half:]))

    @pl.when(b >= nu_ref[0])
    def _():
        o_ref[...] = jnp.zeros(o_ref.shape, o_ref.dtype)


def _ffn(xs, block_expert, n_used, wg, wu, wd, layer):
    p, half = xs.shape
    d = 2 * half
    nb = p // EXPERT_BLOCK
    ff = EXPERT_FF
    return pl.pallas_call(
        _ffn_kernel,
        out_shape=jax.ShapeDtypeStruct((p, half), jnp.uint32),
        grid_spec=pltpu.PrefetchScalarGridSpec(
            num_scalar_prefetch=2,
            grid=(nb,),
            in_specs=[pl.BlockSpec((EXPERT_BLOCK, half), lambda b, be, nu: (b, 0)),
                      pl.BlockSpec((None, None, d, ff), lambda b, be, nu: (layer, be[b], 0, 0)),
                      pl.BlockSpec((None, None, d, ff), lambda b, be, nu: (layer, be[b], 0, 0)),
                      pl.BlockSpec((None, None, ff, d), lambda b, be, nu: (layer, be[b], 0, 0))],
            out_specs=pl.BlockSpec((EXPERT_BLOCK, half), lambda b, be, nu: (b, 0)),
            scratch_shapes=[pltpu.VMEM((d, ff), BF16), pltpu.VMEM((d, ff), BF16), pltpu.VMEM((ff, d), BF16)],
        ),
        compiler_params=_cparams(("arbitrary",), vmem=56 * 1024 * 1024),
        name="moe_ffn",
    )(block_expert, n_used, xs, wg, wu, wd)


MOE_ROW_TILE = 256


def _pack_bf16_pair(a, b):
    hi = lax.bitcast_convert_type(a.astype(BF16).astype(F32), jnp.uint32)
    lo = lax.bitcast_convert_type(b.astype(BF16).astype(F32), jnp.uint32)
    return hi | (lo >> 16)


def _unpack_bf16_pair(w):
    a = lax.bitcast_convert_type(w & jnp.uint32(0xFFFF0000), F32)
    b = lax.bitcast_convert_type(w << 16, F32)
    return a, b


def _dispatch_kernel(dest_ref, h_ref, init_hbm, xs_hbm, sem):
    del init_hbm
    t = MOE_ROW_TILE

    def row_copy(r, k):
        return pltpu.make_async_copy(h_ref.at[pl.ds(r, 1)], xs_hbm.at[pl.ds(dest_ref[0, 0, 2 * r + k], 1)], sem)

    def issue(r, carry):
        row_copy(r, 0).start()
        row_copy(r, 1).start()
        return carry

    def drain(r, carry):
        row_copy(r, 0).wait()
        row_copy(r, 1).wait()
        return carry

    lax.fori_loop(0, t, issue, 0, unroll=8)
    lax.fori_loop(0, t, drain, 0, unroll=8)


def _dispatch(h_packed, dest, n_rows):
    s, half = h_packed.shape
    t = MOE_ROW_TILE
    return pl.pallas_call(
        _dispatch_kernel,
        out_shape=jax.ShapeDtypeStruct((n_rows, half), jnp.uint32),
        grid=(s // t,),
        in_specs=[pl.BlockSpec((1, 1, 2 * t), lambda i: (i, 0, 0), memory_space=pltpu.SMEM),
                  pl.BlockSpec((t, half), lambda i: (i, 0)), pl.BlockSpec(memory_space=pl.ANY)],
        out_specs=pl.BlockSpec(memory_space=pl.ANY),
        scratch_shapes=[pltpu.SemaphoreType.DMA(())],
        input_output_aliases={2: 0},
        compiler_params=_cparams(("arbitrary",)),
        name="moe_dispatch",
    )(dest.reshape(s // t, 1, 2 * t), h_packed, jnp.zeros((n_rows, half), jnp.uint32))


def _combine_kernel(dest_ref, x_ref, route_ref, rows_hbm, o_ref, buf, sem):
    t = MOE_ROW_TILE
    half = D_MODEL // 2

    def row_copy(r, k):
        return pltpu.make_async_copy(rows_hbm.at[pl.ds(dest_ref[0, 0, 2 * r + k], 1)], buf.at[pl.ds(k * t + r, 1)], sem)

    def issue(r, carry):
        row_copy(r, 0).start()
        row_copy(r, 1).start()
        return carry

    def drain(r, carry):
        row_copy(r, 0).wait()
        row_copy(r, 1).wait()
        return carry

    lax.fori_loop(0, t, issue, 0, unroll=8)
    lax.fori_loop(0, t, drain, 0, unroll=8)
    g0 = route_ref[:, 2:3]
    g1 = route_ref[:, 3:4]
    a0, b0 = _unpack_bf16_pair(buf[0:t, :])
    a1, b1 = _unpack_bf16_pair(buf[t:2 * t, :])
    o_ref[:, 0:half] = x_ref[:, 0:half] + g0 * a0 + g1 * a1
    o_ref[:, half:] = x_ref[:, half:] + g0 * b0 + g1 * b1


def _combine(x, route, rows, dest):
    s, d = x.shape
    t = MOE_ROW_TILE
    return pl.pallas_call(
        _combine_kernel,
        out_shape=jax.ShapeDtypeStruct((s, d), F32),
        grid=(s // t,),
        in_specs=[pl.BlockSpec((1, 1, 2 * t), lambda i: (i, 0, 0), memory_space=pltpu.SMEM),
                  pl.BlockSpec((t, d), lambda i: (i, 0)), pl.BlockSpec((t, ROUTER_LANES), lambda i: (i, 0)),
                  pl.BlockSpec(memory_space=pl.ANY)],
        out_specs=pl.BlockSpec((t, d), lambda i: (i, 0)),
        scratch_shapes=[pltpu.VMEM((2 * t, d // 2), jnp.uint32), pltpu.SemaphoreType.DMA(())],
        compiler_params=_cparams(("arbitrary",)),
        name="moe_combine",
    )(dest.reshape(s // t, 1, 2 * t), x, route, rows)


def _rope_tables4(s):
    inv = 1.0 / (ROPE_THETA ** (jnp.arange(0, MLA_ROPE, 2, dtype=F32) / MLA_ROPE))
    ang = jnp.arange(s, dtype=F32)[:, None] * inv[None, :]
    cos, sin = jnp.cos(ang), jnp.sin(ang)
    return jnp.tile(cos, (1, 8)), jnp.tile(sin, (1, 8))


def _moe_layer(x, ln_g, w_group, b_group, w_expert, b_expert, wg, wu, wd, layer):
    s, d = x.shape
    ng, ne = N_EXPERT_GROUPS, N_EXPERTS
    w_r = jnp.zeros((d, ROUTER_LANES), F32).at[:, :ng].set(w_group).at[:, ng:ng + ne].set(w_expert)
    w_hi = w_r.astype(BF16)
    w_lo = (w_r - w_hi.astype(F32)).astype(BF16)
    b_r = jnp.zeros((1, ROUTER_LANES), F32).at[0, :ng].set(b_group).at[0, ng:ng + ne].set(b_expert)
    h_packed, route, cnt = _router(x, ln_g, w_hi, w_lo, b_r)
    e_id = route[:, 0:2].astype(jnp.int32)
    rank = route[:, 4:6].astype(jnp.int32)
    counts = cnt[0, ng:ng + ne].astype(jnp.int32)
    n_blocks = -(-(2 * s + ne * (EXPERT_BLOCK - 1)) // EXPERT_BLOCK)
    padded = (counts + EXPERT_BLOCK - 1) // EXPERT_BLOCK * EXPERT_BLOCK
    pad_ends = jnp.cumsum(padded)
    pad_starts = pad_ends - padded
    dest = pad_starts[e_id] + rank
    block_starts = jnp.arange(n_blocks, dtype=jnp.int32) * EXPERT_BLOCK
    block_expert = jnp.minimum(jnp.sum(pad_ends[None, :] <= block_starts[:, None], axis=1), ne - 1).astype(jnp.int32)
    n_used = (pad_ends[-1] // EXPERT_BLOCK).astype(jnp.int32).reshape(1)
    xs = _dispatch(h_packed, dest, n_blocks * EXPERT_BLOCK)
    rows = _ffn(xs, block_expert, n_used, wg, wu, wd, layer)
    return _combine(x, route, rows, dest)


def _perm_w_in(w_in):
    sp = np.cumsum((512, 128, 128, 512, 512, 512, 256, 64, 512))
    a_q, a_k, a_v, b_a, b_g, c_q, c_kv, c_kr, d_u = jnp.split(w_in, [int(v) for v in sp[:-1]], axis=-1)
    zero = jnp.zeros((w_in.shape[0], 64), w_in.dtype)
    return jnp.concatenate([a_q, a_k, a_v, b_a, b_g, c_q, c_kv, c_kr, zero, d_u], axis=-1)


def _perm_heads(w, parts):
    k = w.shape[0]
    wh = w.reshape(k, MLA_HEADS, sum(parts))
    offs = np.cumsum((0,) + tuple(parts))
    return jnp.concatenate([wh[:, :, offs[j]:offs[j + 1]].reshape(k, -1) for j in range(len(parts))], axis=-1)


def kernel(x, mem, rel_bias, ln_mix_g, w_in, swa_qn_g, swa_kn_g, swa_sinks, conv_dw_w, conv_dw_b, conv_ln_g, conv_ln_b, conv_pw_w, conv_pw_b, mla_qa_g, mla_kva_g, mla_wq_b, mla_wkv_b, mla_qn_nope_g, mla_qn_rope_g, mla_kn_nope_g, mla_kn_rope_g, ssm_a_re, ssm_a_im, ssm_log_dt, ssm_b_re, ssm_b_im, ssm_c_re, ssm_c_im, ssm_d, ssm_glu_w, ssm_glu_b, mix_out_g, w_out, ln_cross_g, mem_ln_g, xa_wq, xa_wk, xa_wv, xa_qn_g, xa_kn_g, xa_wo, ln_moe_g, moe_w_group, moe_b_group, moe_w_expert, moe_b_expert, moe_w_gate, moe_w_up, moe_w_down):
    b, s, d = x.shape
    assert b == 1 and d == D_MODEL and s % FLASH_BQ == 0
    xf = x.reshape(s, d)
    memf = mem.reshape(mem.shape[1], d)
    cos4, sin4 = _rope_tables4(s)
    bias_tab = _bias_table(rel_bias)
    nch = s // SSM_CHUNK
    g_n, n_in = SSM_GROUPS, SSM_GROUP_IN
    for l in range(DEPTH):
        pa, pb, pc, pd, vt_a = _norm_proj(xf, ln_mix_g[l], _perm_w_in(w_in[l]).astype(BF16), P_SEGS, 256,
                                          wt=w_in[l][:, 640:768].T.astype(BF16))
        o_a = _swa(pa, vt_a, swa_qn_g[l], swa_kn_g[l], swa_sinks[l], bias_tab)
        o_b = _conv(pb, conv_dw_w[l], conv_dw_b[l], conv_ln_g[l], conv_ln_b[l], conv_pw_w[l].astype(BF16),
                    conv_pw_b[l])
        q, k, v = _mla_proj(pc, cos4, sin4, mla_qa_g[l], mla_kva_g[l],
                            _perm_heads(mla_wq_b[l], (MLA_NOPE, MLA_ROPE)).astype(BF16),
                            _perm_heads(mla_wkv_b[l], (MLA_NOPE, MLA_V)).astype(BF16),
                            mla_qn_nope_g[l], jnp.tile(mla_qn_rope_g[l], MLA_HEADS), mla_kn_nope_g[l],
                            jnp.concatenate([mla_kn_rope_g[l], jnp.zeros((64,), F32)]))
        vt = v.reshape(MLA_HEADS, s // FLASH_BK, FLASH_BK, MLA_V).swapaxes(2, 3)
        o_c = _flash(q, k, vt)
        u_r = pd.astype(BF16).reshape(nch, SSM_CHUNK, g_n, n_in).transpose(2, 0, 1, 3).reshape(g_n, nch, -1)
        y_r = _ssm(u_r, ssm_a_re[l], ssm_a_im[l], ssm_log_dt[l], ssm_b_re[l], ssm_b_im[l], ssm_c_re[l],
                   ssm_c_im[l], ssm_d[l])
        y_d = y_r.reshape(g_n, nch, SSM_CHUNK, n_in).transpose(1, 2, 0, 3).reshape(s, GROUP_WIDTH)
        xf = _mix_out(xf, o_a, o_b, o_c, y_d, ssm_glu_w[l].astype(BF16), ssm_glu_b[l], mix_out_g[l],
                      w_out[l].astype(BF16))
        k_mem, vt_mem = _norm_proj(memf, mem_ln_g[l], xa_wk[l].astype(BF16), (XA_HEADS * XA_HEAD_DIM,), memf.shape[0],
                                   wt=xa_wv[l].T.astype(BF16))
        xf = _xattn(xf, ln_cross_g[l], xa_wq[l].astype(BF16), xa_qn_g[l], k_mem, vt_mem, xa_kn_g[l],
                    xa_wo[l].astype(BF16))
        xf = _moe_layer(xf, ln_moe_g[l], moe_w_group[l], moe_b_group[l], moe_w_expert[l], moe_b_expert[l],
                        moe_w_gate, moe_w_up, moe_w_down, l)
    return xf.reshape(b, s, d)
```

```python
import functools
import math

import jax
import jax.numpy as jnp
import numpy as np
from jax import lax
from jax.experimental import pallas as pl
from jax.experimental.pallas import tpu as pltpu

F32 = jnp.float32
BF16 = jnp.bfloat16

D_MODEL = 2048
DEPTH = 4
GROUP_WIDTH = 512
SWA_HEADS = 8
SWA_KV_HEADS = 2
SWA_HEAD_DIM = 64
SWA_BLOCK = 128
CONV_WIDTH = 31
MLA_HEADS = 4
MLA_NOPE = 128
MLA_ROPE = 64
MLA_V = 128
MLA_QK_PAD = 256
ROPE_THETA = 10000.0
SSM_GROUPS = 32
SSM_GROUP_IN = 16
SSM_STATE = 64
SSM_CHUNK = 64
REL_BUCKETS = 32
REL_MAX_DIST = 128
XA_HEADS = 4
XA_HEAD_DIM = 128
N_EXPERT_GROUPS = 4
EXPERTS_PER_GROUP = 8
N_EXPERTS = 32
EXPERT_FF = 512
EXPERT_BLOCK = 256
NORM_EPS = 1e-6
NEG_INF = -1e30
LOG2E = 1.4426950408889634

PA_W = 768
PB_W = 1024
PC_W = 896
PD_W = 512
P_SEGS = (PA_W, PB_W, PC_W, PD_W)

V7X_VMEM_LIMIT = 48 * 1024 * 1024

NT_DIMS = (((1,), (1,)), ((), ()))


def _cparams(sem, vmem=V7X_VMEM_LIMIT):
    return pltpu.CompilerParams(dimension_semantics=sem, vmem_limit_bytes=vmem)


def _const_spec(shape):
    nd = len(shape)
    return pl.BlockSpec(shape, lambda *_: (0,) * nd, pipeline_mode=pl.Buffered(1))


def _rms(x, g):
    return x * lax.rsqrt(jnp.mean(x * x, axis=-1, keepdims=True) + NORM_EPS) * g


def _dot(a, b):
    return jnp.dot(a, b, preferred_element_type=F32)


def _dot_nt(a, b):
    return lax.dot_general(a, b, NT_DIMS, preferred_element_type=F32)


def _head_rms_mxu(x, gains, width):
    n = x.shape[-1]
    same = (lax.broadcasted_iota(jnp.int32, (n, n), 0) // width == lax.broadcasted_iota(jnp.int32, (n, n), 1) // width)
    ones = same.astype(BF16)
    x2 = x * x
    hi = x2.astype(BF16)
    lo = (x2 - hi.astype(F32)).astype(BF16)
    ss = _dot(hi, ones) + _dot(lo, ones)
    return x * lax.rsqrt(ss * (1.0 / width) + NORM_EPS) * gains


def _norm_proj_kernel(x_ref, g_ref, w_ref, *refs, segs, n_t):
    h = _rms(x_ref[...], g_ref[...]).astype(BF16)
    o_refs = refs[1:] if n_t else refs
    off = 0
    for o_ref, n in zip(o_refs, segs):
        o_ref[...] = _dot(h, w_ref[:, off:off + n])
        off += n
    if n_t:
        o_refs[len(segs)][...] = _dot_nt(refs[0][...], h).astype(BF16)


def _norm_proj(x, g, w, segs, tm, wt=None):
    s, d = x.shape
    n = sum(segs)
    n_t = 0 if wt is None else wt.shape[0]
    in_specs = [pl.BlockSpec((tm, d), lambda i: (i, 0)), _const_spec((1, d)), _const_spec((d, n))]
    out_shape = [jax.ShapeDtypeStruct((s, n_i), F32) for n_i in segs]
    out_specs = [pl.BlockSpec((tm, n_i), lambda i: (i, 0)) for n_i in segs]
    args = [x, g.reshape(1, d), w]
    if n_t:
        in_specs.append(_const_spec((n_t, d)))
        out_shape.append(jax.ShapeDtypeStruct((n_t, s), BF16))
        out_specs.append(pl.BlockSpec((n_t, tm), lambda i: (0, i)))
        args.append(wt)
    return pl.pallas_call(
        functools.partial(_norm_proj_kernel, segs=segs, n_t=n_t),
        out_shape=out_shape,
        grid=(s // tm,),
        in_specs=in_specs,
        out_specs=out_specs,
        compiler_params=_cparams(("parallel",)),
        name="norm_proj",
    )(*args)


def _t5_buckets():
    qi = np.arange(SWA_BLOCK)[:, None]
    kj = np.arange(2 * SWA_BLOCK)[None, :]
    dist = SWA_BLOCK + qi - kj
    valid = (dist >= 0) & (dist < SWA_BLOCK)
    max_exact = REL_BUCKETS // 2
    d0 = np.maximum(dist, 0)
    dl = np.maximum(dist, 1).astype(np.float32)
    large = max_exact + (np.log(dl / max_exact) / np.log(REL_MAX_DIST / max_exact)
                         * (REL_BUCKETS - max_exact)).astype(np.int32)
    large = np.minimum(large, REL_BUCKETS - 1)
    bucket = np.where(d0 < max_exact, d0, large).astype(np.int32)
    return bucket, valid.astype(np.int32)


def _bias_kernel(rb_ref, bucket_ref, valid_ref, o_ref):
    b = bucket_ref[...]
    ok = valid_ref[...] > 0
    for h in range(SWA_HEADS):
        acc = jnp.zeros(b.shape, F32)
        for k in range(REL_BUCKETS):
            acc = jnp.where(b == k, rb_ref[k, h], acc)
        o_ref[h] = jnp.where(ok, acc, NEG_INF)


def _bias_table(rel_bias):
    bucket, valid = _t5_buckets()
    tab = pl.pallas_call(
        _bias_kernel,
        out_shape=jax.ShapeDtypeStruct((SWA_HEADS, SWA_BLOCK, 2 * SWA_BLOCK), F32),
        in_specs=[pl.BlockSpec(memory_space=pltpu.SMEM),
                  pl.BlockSpec(memory_space=pltpu.VMEM), pl.BlockSpec(memory_space=pltpu.VMEM)],
        out_specs=pl.BlockSpec(memory_space=pltpu.VMEM),
        name="t5_bias_table",
    )(rel_bias, jnp.asarray(bucket), jnp.asarray(valid))
    grp = SWA_HEADS // SWA_KV_HEADS
    tab = tab.reshape(SWA_KV_HEADS, grp, SWA_BLOCK, 2 * SWA_BLOCK).transpose(0, 3, 1, 2)
    return tab.reshape(SWA_KV_HEADS, 2 * SWA_BLOCK, grp * SWA_BLOCK)


SWA_QB = 4


def _swa_kernel(sink_ref, cur_ref, prev_ref, vt_cur_ref, vt_prev_ref, qg_ref, kg_ref, bias_ref, o_ref, ot_scr):
    i = pl.program_id(0)
    blk, dh = SWA_BLOCK, SWA_HEAD_DIM
    grp = SWA_HEADS // SWA_KV_HEADS
    scale = dh ** -0.5

    qn = _head_rms_mxu(cur_ref[:, 0:512], qg_ref[...], dh) * scale
    k_all = jnp.concatenate([prev_ref[:, 512:640], cur_ref[:, 512:640]], axis=0)
    kn = _head_rms_mxu(k_all, kg_ref[...], dh)
    vt_all = jnp.concatenate([vt_prev_ref[...], vt_cur_ref[...]], axis=1)
    lane_k = lax.broadcasted_iota(jnp.int32, kn.shape, 1)
    lane_q = lax.broadcasted_iota(jnp.int32, (blk, 128), 1)
    key_row = lax.broadcasted_iota(jnp.int32, (2 * blk, grp * blk), 0)
    for kh in range(SWA_KV_HEADS):
        k_one = jnp.where(lane_k // dh == kh, kn, 0.0)
        k_dup = (k_one + pltpu.roll(k_one, dh, 1)).astype(BF16)
        sink = jnp.concatenate([jnp.full((1, blk), sink_ref[kh * grp + g], F32) for g in range(grp)], axis=1)
        for j in range(SWA_QB):
            rows = slice(j * blk, (j + 1) * blk)
            qs = []
            for g in range(grp):
                h = kh * grp + g
                q_pair = qn[rows, (h // 2) * 128:(h // 2 + 1) * 128]
                qs.append(jnp.where(lane_q // dh == h % 2, q_pair, 0.0))
            q_st = jnp.concatenate(qs, axis=0).astype(BF16)
            st = _dot_nt(k_dup[j * blk:(j + 2) * blk], q_st) + bias_ref[kh]
            if j == 0:
                st = jnp.where(jnp.logical_and(i == 0, key_row < blk), NEG_INF, st)
            m = jnp.maximum(jnp.max(st, axis=0, keepdims=True), sink)
            p = jnp.exp(st - m)
            denom = jnp.sum(p, axis=0, keepdims=True) + jnp.exp(sink - m)
            ot = _dot(vt_all[kh * dh:(kh + 1) * dh, j * blk:(j + 2) * blk], p.astype(BF16)) / denom
            for g in range(grp):
                h = kh * grp + g
                ot_scr[h * dh:(h + 1) * dh, rows] = ot[:, g * blk:(g + 1) * blk]
    o_ref[...] = ot_scr[...].T


def _swa(pa, vt, qn_g, kn_g, sinks, bias_t):
    s = pa.shape[0]
    tq = SWA_QB * SWA_BLOCK
    grp = SWA_HEADS // SWA_KV_HEADS
    prev = lambda i, sk: (jnp.maximum(i * SWA_QB - 1, 0), 0)
    return pl.pallas_call(
        _swa_kernel,
        out_shape=jax.ShapeDtypeStruct((s, GROUP_WIDTH), F32),
        grid_spec=pltpu.PrefetchScalarGridSpec(
            num_scalar_prefetch=1,
            grid=(s // tq,),
            in_specs=[pl.BlockSpec((tq, PA_W), lambda i, sk: (i, 0)),
                      pl.BlockSpec((SWA_BLOCK, PA_W), prev),
                      pl.BlockSpec((128, tq), lambda i, sk: (0, i)),
                      pl.BlockSpec((128, SWA_BLOCK), lambda i, sk: (0, jnp.maximum(i * SWA_QB - 1, 0))),
                      _const_spec((1, GROUP_WIDTH)), _const_spec((1, 128)),
                      _const_spec((SWA_KV_HEADS, 2 * SWA_BLOCK, grp * SWA_BLOCK))],
            out_specs=pl.BlockSpec((tq, GROUP_WIDTH), lambda i, sk: (i, 0)),
            scratch_shapes=[pltpu.VMEM((GROUP_WIDTH, tq), F32)],
        ),
        compiler_params=_cparams(("parallel",)),
        name="swa_attention",
    )(sinks, pa, pa, vt, vt, jnp.tile(qn_g, SWA_HEADS).reshape(1, -1), jnp.tile(kn_g, SWA_KV_HEADS).reshape(1, -1),
      bias_t)


CONV_TILE = 256
CONV_HALO = 32


def _conv_kernel(cur_ref, prev_ref, dww_ref, dwb_ref, lng_ref, lnb_ref, pww_ref, pwb_ref, o_ref, z_scr, zph_scr):
    i = pl.program_id(0)
    t, c = CONV_TILE, GROUP_WIDTH
    z_scr[CONV_HALO:CONV_HALO + t, :] = cur_ref[:, 0:c] * jax.nn.sigmoid(cur_ref[:, c:2 * c])
    zp = prev_ref[:, 0:c] * jax.nn.sigmoid(prev_ref[:, c:2 * c])
    z_scr[0:CONV_HALO, :] = jnp.where(i > 0, zp, 0.0)
    base = CONV_HALO - (CONV_WIDTH - 1)
    span = t + CONV_HALO - 8
    for ph in range(1, 8):
        zph_scr[ph - 1] = z_scr[ph:ph + span, :]
    acc = jnp.zeros((t, c), F32)
    for k in range(CONV_WIDTH):
        ph, off = (base + k) % 8, (base + k) // 8 * 8
        tap = z_scr[off:off + t, :] if ph == 0 else zph_scr[ph - 1, off:off + t, :]
        acc = acc + dww_ref[k:k + 1, :] * tap
    acc = acc + dwb_ref[...]
    xc = acc - jnp.mean(acc, axis=-1, keepdims=True)
    y = xc * lax.rsqrt(jnp.mean(xc * xc, axis=-1, keepdims=True) + NORM_EPS) * lng_ref[...] + lnb_ref[...]
    y = y * jax.nn.sigmoid(y)
    o_ref[...] = _dot(y.astype(BF16), pww_ref[...]) + pwb_ref[...]


def _conv(pb, dw_w, dw_b, ln_g, ln_b, pw_w, pw_b):
    s = pb.shape[0]
    c = GROUP_WIDTH
    r = CONV_TILE // CONV_HALO
    row = lambda v: v.reshape(1, c)
    return pl.pallas_call(
        _conv_kernel,
        out_shape=jax.ShapeDtypeStruct((s, c), F32),
        grid=(s // CONV_TILE,),
        in_specs=[pl.BlockSpec((CONV_TILE, PB_W), lambda i: (i, 0)),
                  pl.BlockSpec((CONV_HALO, PB_W), lambda i: (jnp.maximum(i * r - 1, 0), 0)),
                  _const_spec((CONV_WIDTH, c)), _const_spec((1, c)), _const_spec((1, c)), _const_spec((1, c)),
                  _const_spec((c, c)), _const_spec((1, c))],
        out_specs=pl.BlockSpec((CONV_TILE, c), lambda i: (i, 0)),
        scratch_shapes=[pltpu.VMEM((CONV_TILE + CONV_HALO, c), F32),
                        pltpu.VMEM((7, CONV_TILE + CONV_HALO - 8, c), F32)],
        compiler_params=_cparams(("parallel",)),
        name="conformer_conv",
    )(pb, pb, dw_w, row(dw_b), row(ln_g), row(ln_b), pw_w, row(pw_b))


MLA_TILE = 256


def _rot_half64(x):
    n = x.shape[-1]
    ax = x.ndim - 1
    lane = lax.broadcasted_iota(jnp.int32, x.shape, ax)
    return jnp.where((lane % 64) < 32, -pltpu.roll(x, n - 32, ax), pltpu.roll(x, 32, ax))


def _seg_rms64(x):
    n = x.shape[-1]
    lane = lax.broadcasted_iota(jnp.int32, x.shape, x.ndim - 1)
    x2 = x * x
    r = jnp.zeros(x.shape, F32)
    for h in range(n // 64):
        msk = jnp.logical_and(lane >= h * 64, lane < (h + 1) * 64)
        sm = jnp.sum(jnp.where(msk, x2, 0.0), axis=-1, keepdims=True)
        r = jnp.where(msk, lax.rsqrt(sm * (1.0 / 64) + NORM_EPS), r)
    return x * r


def _mla_proj_kernel(pc_ref, cos_ref, sin_ref, qag_ref, kvag_ref, wq_ref, wkv_ref,
                     qng_ref, qrg_ref, kng_ref, krg_ref, q_ref, k_ref, v_ref):
    h_n, dn = MLA_HEADS, MLA_NOPE
    scale = (MLA_NOPE + MLA_ROPE) ** -0.5 * LOG2E
    cos4 = cos_ref[...]
    sin4 = sin_ref[...]
    qn = _rms(pc_ref[:, 0:512], qag_ref[...]).astype(BF16)
    q = _dot(qn, wq_ref[...])
    kvn = _rms(pc_ref[:, 512:768], kvag_ref[...]).astype(BF16)
    kv = _dot(kvn, wkv_ref[...])
    qr = _seg_rms64(q[:, h_n * dn:]) * qrg_ref[...]
    qr = (qr * cos4 + _rot_half64(qr) * sin4) * scale
    kr = pc_ref[:, 768:896]
    kr = kr * lax.rsqrt(jnp.sum(kr * kr, axis=-1, keepdims=True) * (1.0 / MLA_ROPE) + NORM_EPS) * krg_ref[...]
    kr = (kr * cos4[:, 0:128] + _rot_half64(kr) * sin4[:, 0:128]).astype(BF16)
    lane = lax.broadcasted_iota(jnp.int32, (q.shape[0], 128), 1)
    for h in range(h_n):
        qnope = _rms(q[:, h * dn:(h + 1) * dn], qng_ref[...]) * scale
        blk = qr[:, (h // 2) * 128:(h // 2 + 1) * 128]
        if h % 2 == 1:
            blk = pltpu.roll(blk, 64, 1)
        blk = jnp.where(lane < 64, blk, 0.0)
        q_ref[h] = jnp.concatenate([qnope, blk], axis=-1).astype(BF16)
        knope = _rms(kv[:, h * dn:(h + 1) * dn], kng_ref[...]).astype(BF16)
        k_ref[h] = jnp.concatenate([knope, kr], axis=-1)
        v_ref[h] = kv[:, (h_n + h) * dn:(h_n + h + 1) * dn].astype(BF16)


def _mla_proj(pc, cos4, sin4, qa_g, kva_g, wq, wkv, qn_nope_g, qn_rope_g4, kn_nope_g, kn_rope_g_pad):
    s = pc.shape[0]
    t = MLA_TILE
    h = MLA_HEADS
    row = lambda v: v.reshape(1, -1)
    return pl.pallas_call(
        _mla_proj_kernel,
        out_shape=[jax.ShapeDtypeStruct((h, s, MLA_QK_PAD), BF16),
                   jax.ShapeDtypeStruct((h, s, MLA_QK_PAD), BF16),
                   jax.ShapeDtypeStruct((h, s, MLA_V), BF16)],
        grid=(s // t,),
        in_specs=[pl.BlockSpec((t, PC_W), lambda i: (i, 0)),
                  pl.BlockSpec((t, 256), lambda i: (i, 0)), pl.BlockSpec((t, 256), lambda i: (i, 0)),
                  _const_spec((1, 512)), _const_spec((1, 256)),
                  _const_spec((512, 768)), _const_spec((256, 1024)),
                  _const_spec((1, 128)), _const_spec((1, 256)), _const_spec((1, 128)), _const_spec((1, 128))],
        out_specs=[pl.BlockSpec((h, t, MLA_QK_PAD), lambda i: (0, i, 0)),
                   pl.BlockSpec((h, t, MLA_QK_PAD), lambda i: (0, i, 0)),
                   pl.BlockSpec((h, t, MLA_V), lambda i: (0, i, 0))],
        compiler_params=_cparams(("parallel",)),
        name="mla_proj",
    )(pc, cos4, sin4, row(qa_g), row(kva_g), wq, wkv, row(qn_nope_g), row(qn_rope_g4), row(kn_nope_g),
      row(kn_rope_g_pad))


FLASH_BQ = 2048
FLASH_BK = 512
FLASH_SUM_ROWS = 16


def _flash_kernel(q_ref, k_ref, vt_ref, o_ref, sa_scr, sb_scr, m_scr, acc_scr):
    bq, bk = FLASH_BQ, FLASH_BK
    per_q = bq // bk
    assert per_q % 2 == 0
    qi = pl.program_id(1)
    m_scr[...] = jnp.full(m_scr.shape, NEG_INF, F32)
    acc_scr[...] = jnp.zeros(acc_scr.shape, F32)
    ones_rows = jnp.ones((FLASH_SUM_ROWS, bk), BF16)

    def scores(j, s_scr, c0=0):
        k = k_ref[pl.ds(pl.multiple_of(j * bk, bk), bk), :]
        s_scr[:, c0:] = _dot_nt(k, q_ref[c0:, :])

    def update(j, s_scr, masked, c0=0, c1=bq):
        st = s_scr[:, c0:c1]
        if masked:
            kpos = j * bk + lax.broadcasted_iota(jnp.int32, st.shape, 0)
            qpos = qi * bq + c0 + lax.broadcasted_iota(jnp.int32, st.shape, 1)
            st = jnp.where(kpos <= qpos, st, NEG_INF)
        m_prev = m_scr[:, c0:c1]
        m_new = jnp.maximum(m_prev, jnp.max(st, axis=0, keepdims=True))
        alpha = jnp.exp2(m_prev - m_new)
        p = jnp.exp2(st - m_new)
        vt_ext = jnp.concatenate([vt_ref[j], ones_rows], axis=0)
        acc_scr[:, c0:c1] = alpha * acc_scr[:, c0:c1] + _dot(vt_ext, p.astype(BF16))
        m_scr[:, c0:c1] = m_new

    scores(0, sa_scr)

    def body(i, carry):
        scores(2 * i + 1, sb_scr)
        update(2 * i, sa_scr, False)
        scores(2 * i + 2, sa_scr)
        update(2 * i + 1, sb_scr, False)
        return carry

    lax.fori_loop(0, qi * (per_q // 2), body, 0)
    first = qi * per_q
    bufs = (sa_scr, sb_scr)
    for c in range(per_q):
        if c + 1 < per_q:
            scores(first + c + 1, bufs[(c + 1) % 2], (c + 1) * bk)
        update(first + c, bufs[c % 2], True, c * bk, (c + 1) * bk)
        if c + 1 < per_q:
            update(first + c, bufs[c % 2], False, (c + 1) * bk, bq)
    o_ref[...] = (acc_scr[0:MLA_V, :] / acc_scr[MLA_V:MLA_V + 1, :]).T


def _flash(q, k, vt):
    h, s, _ = q.shape
    nkb = s // FLASH_BK
    return pl.pallas_call(
        _flash_kernel,
        out_shape=jax.ShapeDtypeStruct((s, h * MLA_V), F32),
        grid=(h, s // FLASH_BQ),
        in_specs=[pl.BlockSpec((None, FLASH_BQ, MLA_QK_PAD), lambda hh, i: (hh, i, 0)),
                  pl.BlockSpec((None, s, MLA_QK_PAD), lambda hh, i: (hh, 0, 0)),
                  pl.BlockSpec((None, nkb, MLA_V, FLASH_BK), lambda hh, i: (hh, 0, 0, 0))],
        out_specs=pl.BlockSpec((FLASH_BQ, MLA_V), lambda hh, i: (i, hh)),
        scratch_shapes=[pltpu.VMEM((FLASH_BK, FLASH_BQ), F32), pltpu.VMEM((FLASH_BK, FLASH_BQ), F32),
                        pltpu.VMEM((1, FLASH_BQ), F32), pltpu.VMEM((MLA_V + FLASH_SUM_ROWS, FLASH_BQ), F32)],
        compiler_params=_cparams(("parallel", "arbitrary"), vmem=56 * 1024 * 1024),
        name="mla_flash",
    )(q, k, vt)


def _split3_dot(a, sel):
    a1 = a.astype(BF16)
    r1 = a - a1.astype(F32)
    a2 = r1.astype(BF16)
    a3 = (r1 - a2.astype(F32)).astype(BF16)
    return _dot(a1, sel) + _dot(a2, sel) + _dot(a3, sel)


def _hi_dot(a, b):
    return lax.dot_general(a, b, (((1,), (0,)), ((), ())), precision=lax.Precision.HIGHEST,
                           preferred_element_type=F32)


def _ssm_kernel(u_ref, arc_ref, aic_ref, arr_ref, air_ref, ldt_ref, bre_ref, bim_ref, btre_ref, btim_ref,
                ctre_ref, ctim_ref, dcol_ref, y_ref, toep_scr, *, nch):
    tc, n_in, n_st = SSM_CHUNK, SSM_GROUP_IN, SSM_STATE
    w = tc * n_in
    dt = jnp.exp(ldt_ref[...])
    ar_c, ai_c = arc_ref[...], aic_ref[...]
    ar_r, ai_r = arr_ref[...], air_ref[...]

    def lam_pow(ar, ai, e):
        mag = jnp.exp(ar * dt * e)
        ang = ai * dt * e
        return mag * jnp.cos(ang), mag * jnp.sin(ang)

    def zoh_coef(ar, ai):
        lr, li = lam_pow(ar, ai, 1.0)
        x, y = lr - 1.0, li
        den = ar * ar + ai * ai
        return (x * ar + y * ai) / den, (y * ar - x * ai) / den

    lane_w = lax.broadcasted_iota(jnp.int32, (tc, w), 1)
    sel_tau = (lane_w // n_in == lax.broadcasted_iota(jnp.int32, (tc, w), 0)).astype(BF16)
    lane_o = lax.broadcasted_iota(jnp.int32, (n_in, w), 1)
    sel_o = (lane_o % n_in == lax.broadcasted_iota(jnp.int32, (n_in, w), 0)).astype(BF16)

    tau = lax.broadcasted_iota(jnp.int32, (n_st, tc), 1).astype(F32)
    lre, lim = lam_pow(ar_c, ai_c, tau)
    lre_x, lim_x = _split3_dot(lre, sel_tau), _split3_dot(lim, sel_tau)
    rre, rim = lam_pow(ar_c, ai_c, (tc - 1.0) - tau)
    rre_x, rim_x = _split3_dot(rre, sel_tau), _split3_dot(rim, sel_tau)
    ctre_x, ctim_x = _split3_dot(ctre_ref[...], sel_o), _split3_dot(ctim_ref[...], sel_o)

    e_re = lre_x * ctre_x - lim_x * ctim_x
    e_im = lre_x * ctim_x + lim_x * ctre_x

    cr_c, ci_c = zoh_coef(ar_c, ai_c)
    bb_re = cr_c * bre_ref[...] - ci_c * bim_ref[...]
    bb_im = cr_c * bim_ref[...] + ci_c * bre_ref[...]
    bb_re_x, bb_im_x = _split3_dot(bb_re, sel_o), _split3_dot(bb_im, sel_o)
    f_re = (rre_x * bb_re_x - rim_x * bb_im_x).astype(BF16)
    f_im = (rre_x * bb_im_x + rim_x * bb_re_x).astype(BF16)

    cr_r, ci_r = zoh_coef(ar_r, ai_r)
    bbt_re = cr_r * btre_ref[...] - ci_r * btim_ref[...]
    bbt_im = cr_r * btim_ref[...] + ci_r * btre_ref[...]
    g_row = _hi_dot(bbt_re, e_re) - _hi_dot(bbt_im, e_im)
    lane_g = lax.broadcasted_iota(jnp.int32, (n_in, w), 1)
    row_g = lax.broadcasted_iota(jnp.int32, (n_in, w), 0)
    g_row = g_row + jnp.where(lane_g == row_g, dcol_ref[...], 0.0)
    for s in range(tc):
        rolled = g_row if s == 0 else pltpu.roll(g_row, n_in * s, 1)
        toep_scr[s * n_in:(s + 1) * n_in, :] = jnp.where(lane_g >= n_in * s, rolled, 0.0).astype(BF16)

    u = u_ref[...]
    s_re = _dot_nt(u, f_re)
    s_im = _dot_nt(u, f_im)
    row = lax.broadcasted_iota(jnp.int32, (nch, n_st), 0)
    sh = 1
    while sh < nch:
        pr, pi = lam_pow(ar_r, ai_r, float(tc * sh))
        re_s = jnp.where(row >= sh, pltpu.roll(s_re, sh, 0), 0.0)
        im_s = jnp.where(row >= sh, pltpu.roll(s_im, sh, 0), 0.0)
        s_re, s_im = s_re + pr * re_s - pi * im_s, s_im + pr * im_s + pi * re_s
        sh *= 2
    p_re = jnp.where(row >= 1, pltpu.roll(s_re, 1, 0), 0.0)
    p_im = jnp.where(row >= 1, pltpu.roll(s_im, 1, 0), 0.0)
    l1r, l1i = lam_pow(ar_c, ai_c, 1.0)
    h_re = (l1r * e_re - l1i * e_im).astype(BF16)
    h_im = (l1r * e_im + l1i * e_re).astype(BF16)
    y = _dot(u, toep_scr[...]) + _dot(p_re.astype(BF16), h_re) - _dot(p_im.astype(BF16), h_im)
    y_ref[...] = y


def _ssm(u_r, a_re, a_im, log_dt, b_re, b_im, c_re, c_im, d_skip):
    g, nch, w = u_r.shape
    n_st, n_in = SSM_STATE, SSM_GROUP_IN
    col = lambda v: v.reshape(g, n_st, 1)
    rw = lambda v: v.reshape(g, 1, n_st)
    spec = lambda *shape: pl.BlockSpec((None,) + shape, lambda i: (i,) + (0,) * len(shape))
    return pl.pallas_call(
        functools.partial(_ssm_kernel, nch=nch),
        out_shape=jax.ShapeDtypeStruct((g, nch, w), F32),
        grid=(g,),
        in_specs=[spec(nch, w), spec(n_st, 1), spec(n_st, 1), spec(1, n_st), spec(1, n_st), spec(1, 1),
                  spec(n_st, n_in), spec(n_st, n_in), spec(n_in, n_st), spec(n_in, n_st),
                  spec(n_st, n_in), spec(n_st, n_in), spec(n_in, 1)],
        out_specs=spec(nch, w),
        scratch_shapes=[pltpu.VMEM((w, w), BF16)],
        compiler_params=_cparams(("parallel",)),
        name="s5_ssm",
    )(u_r, col(a_re), col(a_im), rw(a_re), rw(a_im), log_dt.reshape(g, 1, 1), b_re, b_im,
      jnp.swapaxes(b_re, 1, 2), jnp.swapaxes(b_im, 1, 2), jnp.swapaxes(c_re, 1, 2), jnp.swapaxes(c_im, 1, 2),
      d_skip.reshape(g, n_in, 1))


OUT_TILE = 256


def _mix_out_kernel(x_ref, oa_ref, ob_ref, oc_ref, yd_ref, gw_ref, gb_ref, mg_ref, wo_ref, o_ref):
    c = GROUP_WIDTH
    y = jax.nn.gelu(yd_ref[...])
    od = y * jax.nn.sigmoid(_dot(y.astype(BF16), gw_ref[...]) + gb_ref[...])
    acc = x_ref[...]
    for m, o in enumerate((oa_ref[...], ob_ref[...], oc_ref[...], od)):
        n = _rms(o, mg_ref[:, m * c:(m + 1) * c]).astype(BF16)
        acc = acc + _dot(n, wo_ref[m * c:(m + 1) * c, :])
    o_ref[...] = acc


def _mix_out(x, oa, ob, oc, yd, glu_w, glu_b, mix_g, w_out):
    s, d = x.shape
    c = GROUP_WIDTH
    t = OUT_TILE
    tile = lambda n: pl.BlockSpec((t, n), lambda i: (i, 0))
    return pl.pallas_call(
        _mix_out_kernel,
        out_shape=jax.ShapeDtypeStruct((s, d), F32),
        grid=(s // t,),
        in_specs=[tile(d), tile(c), tile(c), tile(c), tile(c),
                  _const_spec((c, c)), _const_spec((1, c)), _const_spec((1, d)), _const_spec((d, d))],
        out_specs=tile(d),
        compiler_params=_cparams(("parallel",)),
        name="mix_out",
    )(x, oa, ob, oc, yd, glu_w, glu_b.reshape(1, c), mix_g.reshape(1, d), w_out)


XA_TILE = 256


def _xattn_kernel(x_ref, lg_ref, wq_ref, qg_ref, k_ref, vt_ref, kg_ref, wo_ref, o_ref):
    dh = XA_HEAD_DIM
    x = x_ref[...]
    q = _dot(_rms(x, lg_ref[...]).astype(BF16), wq_ref[...])
    qn = (_head_rms_mxu(q, qg_ref[...], dh) * dh ** -0.5).astype(BF16)
    kn = _head_rms_mxu(k_ref[...], kg_ref[...], dh).astype(BF16)
    outs = []
    for h in range(XA_HEADS):
        cols = slice(h * dh, (h + 1) * dh)
        st = _dot_nt(kn[:, cols], qn[:, cols])
        p = jnp.exp(st - jnp.max(st, axis=0, keepdims=True))
        outs.append(_dot(vt_ref[cols, :], p.astype(BF16)) / jnp.sum(p, axis=0, keepdims=True))
    o = jnp.concatenate(outs, axis=0).T
    o_ref[...] = x + _dot(o.astype(BF16), wo_ref[...])


def _xattn(x, ln_g, wq, qn_g, k_mem, vt_mem, kn_g, wo):
    s, d = x.shape
    m, hw = k_mem.shape
    t = XA_TILE
    return pl.pallas_call(
        _xattn_kernel,
        out_shape=jax.ShapeDtypeStruct((s, d), F32),
        grid=(s // t,),
        in_specs=[pl.BlockSpec((t, d), lambda i: (i, 0)), _const_spec((1, d)), _const_spec((d, hw)),
                  _const_spec((1, hw)), _const_spec((m, hw)), _const_spec((hw, m)), _const_spec((1, hw)),
                  _const_spec((hw, d))],
        out_specs=pl.BlockSpec((t, d), lambda i: (i, 0)),
        compiler_params=_cparams(("parallel",)),
        name="mem_xattn",
    )(x, ln_g.reshape(1, d), wq, jnp.tile(qn_g, XA_HEADS).reshape(1, -1), k_mem, vt_mem,
      jnp.tile(kn_g, XA_HEADS).reshape(1, -1), wo)


ROUTER_TILE = 256
ROUTER_LANES = 128


def _router_kernel(x_ref, lg_ref, whi_ref, wlo_ref, b_ref, h_ref, r_ref, cnt_ref, carry_scr):
    i = pl.program_id(0)
    t = ROUTER_TILE
    ng, epg = N_EXPERT_GROUPS, EXPERTS_PER_GROUP

    @pl.when(i == 0)
    def _():
        carry_scr[...] = jnp.zeros(carry_scr.shape, F32)

    h = _rms(x_ref[...], lg_ref[...])
    h_hi = h.astype(BF16)
    h_lo = (h - h_hi.astype(F32)).astype(BF16)
    _rows_store(h_ref, 0, t, _pack_bf16_pair(h[:, 0:D_MODEL // 2], h[:, D_MODEL // 2:]))
    logits = _dot(h_hi, whi_ref[...]) + _dot(h_hi, wlo_ref[...]) + _dot(h_lo, whi_ref[...]) + b_ref[...]
    lane = lax.broadcasted_iota(jnp.int32, (t, ROUTER_LANES), 1)
    big = jnp.int32(ROUTER_LANES)

    def first_max(vals, mask):
        v = jnp.where(mask, vals, -jnp.inf)
        mx = jnp.max(v, axis=-1, keepdims=True)
        idx = jnp.min(jnp.where(jnp.logical_and(mask, v == mx), lane, big), axis=-1, keepdims=True)
        return mx, idx

    gmask = lane < ng
    gmax, gidx = first_max(logits, gmask)
    g_w = 1.0 / jnp.sum(jnp.where(gmask, jnp.exp(logits - gmax), 0.0), axis=-1, keepdims=True)
    emask = jnp.logical_and(lane >= ng + gidx * epg, lane < ng + (gidx + 1) * epg)
    v1, i1 = first_max(logits, emask)
    v2, i2 = first_max(logits, jnp.logical_and(emask, lane != i1))
    e2 = jnp.exp(v2 - v1)
    gate1 = g_w / (1.0 + e2)
    gate2 = g_w * e2 / (1.0 + e2)

    oh1 = lane == i1
    oh2 = lane == i2
    both = (oh1.astype(F32) + oh2.astype(F32))
    tri = (lax.broadcasted_iota(jnp.int32, (t, t), 1) < lax.broadcasted_iota(jnp.int32, (t, t), 0)).astype(BF16)
    before = _dot(tri, both.astype(BF16)) + carry_scr[...]
    rank1 = jnp.sum(jnp.where(oh1, before, 0.0), axis=-1, keepdims=True)
    rank2 = jnp.sum(jnp.where(oh2, before, 0.0), axis=-1, keepdims=True)
    carry_scr[...] = carry_scr[...] + jnp.sum(both, axis=0, keepdims=True)
    cnt_ref[...] = jnp.broadcast_to(carry_scr[...], cnt_ref.shape)

    out = jnp.zeros((t, ROUTER_LANES), F32)
    for k, val in enumerate(((i1 - ng).astype(F32), (i2 - ng).astype(F32), gate1, gate2, rank1, rank2)):
        out = jnp.where(lane == k, val, out)
    r_ref[...] = out


def _router(x, ln_g, w_hi, w_lo, bias):
    s, d = x.shape
    t = ROUTER_TILE
    return pl.pallas_call(
        _router_kernel,
        out_shape=[jax.ShapeDtypeStruct((s * ROW_SUB, 128), jnp.uint32), jax.ShapeDtypeStruct((s, ROUTER_LANES), F32),
                   jax.ShapeDtypeStruct((8, ROUTER_LANES), F32)],
        grid=(s // t,),
        in_specs=[pl.BlockSpec((t, d), lambda i: (i, 0)), _const_spec((1, d)),
                  _const_spec((d, ROUTER_LANES)), _const_spec((d, ROUTER_LANES)), _const_spec((1, ROUTER_LANES))],
        out_specs=[pl.BlockSpec((t * ROW_SUB, 128), lambda i: (i, 0)), pl.BlockSpec((t, ROUTER_LANES), lambda i: (i, 0)),
                   pl.BlockSpec((8, ROUTER_LANES), lambda i: (0, 0))],
        scratch_shapes=[pltpu.VMEM((1, ROUTER_LANES), F32)],
        compiler_params=_cparams(("arbitrary",)),
        name="moe_router",
    )(x, ln_g.reshape(1, d), w_hi, w_lo, bias)


def _ffn_kernel(be_ref, nu_ref, nx_ref, x_ref, wg_hbm, wu_hbm, wd_hbm, o_ref,
                wg_st, wu_st, wd_st, wg_scr, wu_scr, wd_scr, sem, *, layer):
    b = pl.program_id(0)
    half = D_MODEL // 2
    e = be_ref[b]
    prev = be_ref[jnp.maximum(b - 1, 0)]

    def fetch(expert):
        return (pltpu.make_async_copy(wg_hbm.at[layer, expert], wg_st, sem.at[0]),
                pltpu.make_async_copy(wu_hbm.at[layer, expert], wu_st, sem.at[1]),
                pltpu.make_async_copy(wd_hbm.at[layer, expert], wd_st, sem.at[2]))

    @pl.when(b == 0)
    def _():
        for c in fetch(e):
            c.start()

    @pl.when(jnp.logical_or(b == 0, e != prev))
    def _():
        for c in fetch(e):
            c.wait()
        wg_scr[...] = wg_st[...].astype(BF16)
        wu_scr[...] = wu_st[...].astype(BF16)
        wd_scr[...] = wd_st[...].astype(BF16)
        nxt = nx_ref[e]

        @pl.when(nxt != e)
        def _():
            for c in fetch(nxt):
                c.start()

    @pl.when(b < nu_ref[0])
    def _():
        xa, xb = _unpack_bf16_pair(_rows_load(x_ref, 0, EXPERT_BLOCK))
        xa, xb = xa.astype(BF16), xb.astype(BF16)
        g = _dot(xa, wg_scr[0:half, :]) + _dot(xb, wg_scr[half:, :])
        u = _dot(xa, wu_scr[0:half, :]) + _dot(xb, wu_scr[half:, :])
        hid = ((g * jax.nn.sigmoid(g)) * u).astype(BF16)
        _rows_store(o_ref, 0, EXPERT_BLOCK, _pack_bf16_pair(_dot(hid, wd_scr[:, 0:half]), _dot(hid, wd_scr[:, half:])))

    @pl.when(b >= nu_ref[0])
    def _():
        o_ref[...] = jnp.zeros(o_ref.shape, o_ref.dtype)


def _ffn(xs, block_expert, n_used, next_expert, wg, wu, wd, layer):
    d = D_MODEL
    nb = xs.shape[0] // (EXPERT_BLOCK * ROW_SUB)
    blk = (EXPERT_BLOCK * ROW_SUB, 128)
    ff = EXPERT_FF
    return pl.pallas_call(
        functools.partial(_ffn_kernel, layer=layer),
        out_shape=jax.ShapeDtypeStruct(xs.shape, jnp.uint32),
        grid_spec=pltpu.PrefetchScalarGridSpec(
            num_scalar_prefetch=3,
            grid=(nb,),
            in_specs=[pl.BlockSpec(blk, lambda b, be, nu, nx: (b, 0)),
                      pl.BlockSpec(memory_space=pl.ANY), pl.BlockSpec(memory_space=pl.ANY),
                      pl.BlockSpec(memory_space=pl.ANY)],
            out_specs=pl.BlockSpec(blk, lambda b, be, nu, nx: (b, 0)),
            scratch_shapes=[pltpu.VMEM((d, ff), F32), pltpu.VMEM((d, ff), F32), pltpu.VMEM((ff, d), F32),
                            pltpu.VMEM((d, ff), BF16), pltpu.VMEM((d, ff), BF16), pltpu.VMEM((ff, d), BF16),
                            pltpu.SemaphoreType.DMA((3,))],
        ),
        compiler_params=_cparams(("arbitrary",)),
        name="moe_ffn",
    )(block_expert, n_used, next_expert, xs, wg, wu, wd)


MOE_ROW_TILE = 256


ROW_SUB = 8


def _rows_store(ref, base, n, packed):
    for sub in range(ROW_SUB):
        ref[pl.ds(base * ROW_SUB + sub, n, stride=ROW_SUB), :] = packed[:, sub * 128:(sub + 1) * 128]


def _rows_load(ref, base, n):
    return jnp.concatenate([ref[pl.ds(base * ROW_SUB + sub, n, stride=ROW_SUB), :] for sub in range(ROW_SUB)], axis=1)


def _row_tile(idx):
    return pl.ds(pl.multiple_of(idx * ROW_SUB, ROW_SUB), ROW_SUB)


def _pack_bf16_pair(a, b):
    hi = lax.bitcast_convert_type(a.astype(BF16).astype(F32), jnp.uint32)
    lo = lax.bitcast_convert_type(b.astype(BF16).astype(F32), jnp.uint32)
    return hi | (lo >> 16)


def _unpack_bf16_pair(w):
    a = lax.bitcast_convert_type(w & jnp.uint32(0xFFFF0000), F32)
    b = lax.bitcast_convert_type(w << 16, F32)
    return a, b


def _dispatch_kernel(dest_ref, h_ref, init_hbm, xs_hbm, sem):
    del init_hbm
    t = MOE_ROW_TILE

    def row_copy(r, k):
        return pltpu.make_async_copy(h_ref.at[_row_tile(r)], xs_hbm.at[_row_tile(dest_ref[0, 0, 2 * r + k])], sem)

    def issue(r, carry):
        row_copy(r, 0).start()
        row_copy(r, 1).start()
        return carry

    def drain(r, carry):
        row_copy(r, 0).wait()
        row_copy(r, 1).wait()
        return carry

    lax.fori_loop(0, t, issue, 0, unroll=8)
    lax.fori_loop(0, t, drain, 0, unroll=8)


def _dispatch(h_packed, dest, n_rows):
    s = h_packed.shape[0] // ROW_SUB
    t = MOE_ROW_TILE
    return pl.pallas_call(
        _dispatch_kernel,
        out_shape=jax.ShapeDtypeStruct((n_rows * ROW_SUB, 128), jnp.uint32),
        grid=(s // t,),
        in_specs=[pl.BlockSpec((1, 1, 2 * t), lambda i: (i, 0, 0), memory_space=pltpu.SMEM),
                  pl.BlockSpec((t * ROW_SUB, 128), lambda i: (i, 0)), pl.BlockSpec(memory_space=pl.ANY)],
        out_specs=pl.BlockSpec(memory_space=pl.ANY),
        scratch_shapes=[pltpu.SemaphoreType.DMA(())],
        input_output_aliases={2: 0},
        compiler_params=_cparams(("arbitrary",)),
        name="moe_dispatch",
    )(dest.reshape(s // t, 1, 2 * t), h_packed, jnp.zeros((n_rows * ROW_SUB, 128), jnp.uint32))


def _combine_kernel(dest_ref, x_ref, route_ref, rows_hbm, o_ref, buf, sem):
    t = MOE_ROW_TILE
    half = D_MODEL // 2

    def row_copy(r, k):
        return pltpu.make_async_copy(rows_hbm.at[_row_tile(dest_ref[0, 0, 2 * r + k])], buf.at[_row_tile(k * t + r)], sem)

    def issue(r, carry):
        row_copy(r, 0).start()
        row_copy(r, 1).start()
        return carry

    def drain(r, carry):
        row_copy(r, 0).wait()
        row_copy(r, 1).wait()
        return carry

    lax.fori_loop(0, t, issue, 0, unroll=8)
    lax.fori_loop(0, t, drain, 0, unroll=8)
    g0 = route_ref[:, 2:3]
    g1 = route_ref[:, 3:4]
    a0, b0 = _unpack_bf16_pair(_rows_load(buf, 0, t))
    a1, b1 = _unpack_bf16_pair(_rows_load(buf, t, t))
    o_ref[:, 0:half] = x_ref[:, 0:half] + g0 * a0 + g1 * a1
    o_ref[:, half:] = x_ref[:, half:] + g0 * b0 + g1 * b1


def _combine(x, route, rows, dest):
    s, d = x.shape
    t = MOE_ROW_TILE
    return pl.pallas_call(
        _combine_kernel,
        out_shape=jax.ShapeDtypeStruct((s, d), F32),
        grid=(s // t,),
        in_specs=[pl.BlockSpec((1, 1, 2 * t), lambda i: (i, 0, 0), memory_space=pltpu.SMEM),
                  pl.BlockSpec((t, d), lambda i: (i, 0)), pl.BlockSpec((t, ROUTER_LANES), lambda i: (i, 0)),
                  pl.BlockSpec(memory_space=pl.ANY)],
        out_specs=pl.BlockSpec((t, d), lambda i: (i, 0)),
        scratch_shapes=[pltpu.VMEM((2 * t * ROW_SUB, 128), jnp.uint32), pltpu.SemaphoreType.DMA(())],
        compiler_params=_cparams(("arbitrary",)),
        name="moe_combine",
    )(dest.reshape(s // t, 1, 2 * t), x, route, rows)


def _rope_tables4(s):
    inv = 1.0 / (ROPE_THETA ** (jnp.arange(0, MLA_ROPE, 2, dtype=F32) / MLA_ROPE))
    ang = jnp.arange(s, dtype=F32)[:, None] * inv[None, :]
    cos, sin = jnp.cos(ang), jnp.sin(ang)
    return jnp.tile(cos, (1, 8)), jnp.tile(sin, (1, 8))


def _moe_layer(x, ln_g, w_group, b_group, w_expert, b_expert, wg, wu, wd, layer):
    s, d = x.shape
    ng, ne = N_EXPERT_GROUPS, N_EXPERTS
    w_r = jnp.zeros((d, ROUTER_LANES), F32).at[:, :ng].set(w_group).at[:, ng:ng + ne].set(w_expert)
    w_hi = w_r.astype(BF16)
    w_lo = (w_r - w_hi.astype(F32)).astype(BF16)
    b_r = jnp.zeros((1, ROUTER_LANES), F32).at[0, :ng].set(b_group).at[0, ng:ng + ne].set(b_expert)
    h_packed, route, cnt = _router(x, ln_g, w_hi, w_lo, b_r)
    e_id = route[:, 0:2].astype(jnp.int32)
    rank = route[:, 4:6].astype(jnp.int32)
    counts = cnt[0, ng:ng + ne].astype(jnp.int32)
    n_blocks = -(-(2 * s + ne * (EXPERT_BLOCK - 1)) // EXPERT_BLOCK)
    padded = (counts + EXPERT_BLOCK - 1) // EXPERT_BLOCK * EXPERT_BLOCK
    pad_ends = jnp.cumsum(padded)
    pad_starts = pad_ends - padded
    dest = pad_starts[e_id] + rank
    block_starts = jnp.arange(n_blocks, dtype=jnp.int32) * EXPERT_BLOCK
    block_expert = jnp.minimum(jnp.sum(pad_ends[None, :] <= block_starts[:, None], axis=1), ne - 1).astype(jnp.int32)
    n_used = (pad_ends[-1] // EXPERT_BLOCK).astype(jnp.int32).reshape(1)
    xs = _dispatch(h_packed, dest, n_blocks * EXPERT_BLOCK)
    experts = jnp.arange(ne, dtype=jnp.int32)
    present = jnp.any(block_expert[None, :] == experts[:, None], axis=1)
    later = jnp.where(jnp.logical_and(present[None, :], experts[None, :] > experts[:, None]), experts[None, :], ne)
    next_expert = jnp.where(jnp.min(later, axis=1) < ne, jnp.min(later, axis=1), experts).astype(jnp.int32)
    rows = _ffn(xs, block_expert, n_used, next_expert, wg, wu, wd, layer)
    return _combine(x, route, rows, dest)


def _perm_w_in(w_in):
    sp = np.cumsum((512, 128, 128, 512, 512, 512, 256, 64, 512))
    a_q, a_k, a_v, b_a, b_g, c_q, c_kv, c_kr, d_u = jnp.split(w_in, [int(v) for v in sp[:-1]], axis=-1)
    zero = jnp.zeros((w_in.shape[0], 64), w_in.dtype)
    return jnp.concatenate([a_q, a_k, a_v, b_a, b_g, c_q, c_kv, c_kr, zero, d_u], axis=-1)


def _perm_heads(w, parts):
    k = w.shape[0]
    wh = w.reshape(k, MLA_HEADS, sum(parts))
    offs = np.cumsum((0,) + tuple(parts))
    return jnp.concatenate([wh[:, :, offs[j]:offs[j + 1]].reshape(k, -1) for j in range(len(parts))], axis=-1)


def kernel(x, mem, rel_bias, ln_mix_g, w_in, swa_qn_g, swa_kn_g, swa_sinks, conv_dw_w, conv_dw_b, conv_ln_g, conv_ln_b, conv_pw_w, conv_pw_b, mla_qa_g, mla_kva_g, mla_wq_b, mla_wkv_b, mla_qn_nope_g, mla_qn_rope_g, mla_kn_nope_g, mla_kn_rope_g, ssm_a_re, ssm_a_im, ssm_log_dt, ssm_b_re, ssm_b_im, ssm_c_re, ssm_c_im, ssm_d, ssm_glu_w, ssm_glu_b, mix_out_g, w_out, ln_cross_g, mem_ln_g, xa_wq, xa_wk, xa_wv, xa_qn_g, xa_kn_g, xa_wo, ln_moe_g, moe_w_group, moe_b_group, moe_w_expert, moe_b_expert, moe_w_gate, moe_w_up, moe_w_down):
    b, s, d = x.shape
    assert b == 1 and d == D_MODEL and s % FLASH_BQ == 0
    xf = x.reshape(s, d)
    memf = mem.reshape(mem.shape[1], d)
    cos4, sin4 = _rope_tables4(s)
    bias_tab = _bias_table(rel_bias)
    nch = s // SSM_CHUNK
    g_n, n_in = SSM_GROUPS, SSM_GROUP_IN
    for l in range(DEPTH):
        pa, pb, pc, pd, vt_a = _norm_proj(xf, ln_mix_g[l], _perm_w_in(w_in[l]).astype(BF16), P_SEGS, 256,
                                          wt=w_in[l][:, 640:768].T.astype(BF16))
        o_a = _swa(pa, vt_a, swa_qn_g[l], swa_kn_g[l], swa_sinks[l], bias_tab)
        o_b = _conv(pb, conv_dw_w[l], conv_dw_b[l], conv_ln_g[l], conv_ln_b[l], conv_pw_w[l].astype(BF16),
                    conv_pw_b[l])
        q, k, v = _mla_proj(pc, cos4, sin4, mla_qa_g[l], mla_kva_g[l],
                            _perm_heads(mla_wq_b[l], (MLA_NOPE, MLA_ROPE)).astype(BF16),
                            _perm_heads(mla_wkv_b[l], (MLA_NOPE, MLA_V)).astype(BF16),
                            mla_qn_nope_g[l], jnp.tile(mla_qn_rope_g[l], MLA_HEADS), mla_kn_nope_g[l],
                            jnp.concatenate([mla_kn_rope_g[l], jnp.zeros((64,), F32)]))
        vt = v.reshape(MLA_HEADS, s // FLASH_BK, FLASH_BK, MLA_V).swapaxes(2, 3)
        o_c = _flash(q, k, vt)
        u_r = pd.astype(BF16).reshape(nch, SSM_CHUNK, g_n, n_in).transpose(2, 0, 1, 3).reshape(g_n, nch, -1)
        y_r = _ssm(u_r, ssm_a_re[l], ssm_a_im[l], ssm_log_dt[l], ssm_b_re[l], ssm_b_im[l], ssm_c_re[l],
                   ssm_c_im[l], ssm_d[l])
        y_d = y_r.reshape(g_n, nch, SSM_CHUNK, n_in).transpose(1, 2, 0, 3).reshape(s, GROUP_WIDTH)
        xf = _mix_out(xf, o_a, o_b, o_c, y_d, ssm_glu_w[l].astype(BF16), ssm_glu_b[l], mix_out_g[l],
                      w_out[l].astype(BF16))
        k_mem, vt_mem = _norm_proj(memf, mem_ln_g[l], xa_wk[l].astype(BF16), (XA_HEADS * XA_HEAD_DIM,), memf.shape[0],
                                   wt=xa_wv[l].T.astype(BF16))
        xf = _xattn(xf, ln_cross_g[l], xa_wq[l].astype(BF16), xa_qn_g[l], k_mem, vt_mem, xa_kn_g[l],
                    xa_wo[l].astype(BF16))
        xf = _moe_layer(xf, ln_moe_g[l], moe_w_group[l], moe_b_group[l], moe_w_expert[l], moe_b_expert[l],
                        moe_w_gate, moe_w_up, moe_w_down, l)
    return xf.reshape(b, s, d)
```

```python
import functools
import math

import jax
import jax.numpy as jnp
import numpy as np
from jax import lax
from jax.experimental import pallas as pl
from jax.experimental.pallas import tpu as pltpu

F32 = jnp.float32
BF16 = jnp.bfloat16

D_MODEL = 2048
DEPTH = 4
GROUP_WIDTH = 512
SWA_HEADS = 8
SWA_KV_HEADS = 2
SWA_HEAD_DIM = 64
SWA_BLOCK = 128
CONV_WIDTH = 31
MLA_HEADS = 4
MLA_NOPE = 128
MLA_ROPE = 64
MLA_V = 128
MLA_QK_PAD = 256
ROPE_THETA = 10000.0
SSM_GROUPS = 32
SSM_GROUP_IN = 16
SSM_STATE = 64
SSM_CHUNK = 64
REL_BUCKETS = 32
REL_MAX_DIST = 128
XA_HEADS = 4
XA_HEAD_DIM = 128
N_EXPERT_GROUPS = 4
EXPERTS_PER_GROUP = 8
N_EXPERTS = 32
EXPERT_FF = 512
EXPERT_BLOCK = 256
NORM_EPS = 1e-6
NEG_INF = -1e30
LOG2E = 1.4426950408889634

PA_W = 768
PB_W = 1024
PC_W = 896
PD_W = 512
P_SEGS = (PA_W, PB_W, PC_W, PD_W)

V7X_VMEM_LIMIT = 48 * 1024 * 1024

NT_DIMS = (((1,), (1,)), ((), ()))


def _cparams(sem, vmem=V7X_VMEM_LIMIT):
    return pltpu.CompilerParams(dimension_semantics=sem, vmem_limit_bytes=vmem)


def _const_spec(shape):
    nd = len(shape)
    return pl.BlockSpec(shape, lambda *_: (0,) * nd, pipeline_mode=pl.Buffered(1))


def _rms(x, g):
    return x * lax.rsqrt(jnp.mean(x * x, axis=-1, keepdims=True) + NORM_EPS) * g


def _dot(a, b):
    return jnp.dot(a, b, preferred_element_type=F32)


def _dot_nt(a, b):
    return lax.dot_general(a, b, NT_DIMS, preferred_element_type=F32)


def _head_rms_mxu(x, gains, width):
    n = x.shape[-1]
    same = (lax.broadcasted_iota(jnp.int32, (n, n), 0) // width == lax.broadcasted_iota(jnp.int32, (n, n), 1) // width)
    ones = same.astype(BF16)
    x2 = x * x
    hi = x2.astype(BF16)
    lo = (x2 - hi.astype(F32)).astype(BF16)
    ss = _dot(hi, ones) + _dot(lo, ones)
    return x * lax.rsqrt(ss * (1.0 / width) + NORM_EPS) * gains


def _norm_proj_kernel(x_ref, g_ref, w_ref, *refs, segs, n_t):
    h = _rms(x_ref[...], g_ref[...]).astype(BF16)
    o_refs = refs[1:] if n_t else refs
    off = 0
    for o_ref, n in zip(o_refs, segs):
        o_ref[...] = _dot(h, w_ref[:, off:off + n])
        off += n
    if n_t:
        o_refs[len(segs)][...] = _dot_nt(refs[0][...], h).astype(BF16)


def _norm_proj(x, g, w, segs, tm, wt=None):
    s, d = x.shape
    n = sum(segs)
    n_t = 0 if wt is None else wt.shape[0]
    in_specs = [pl.BlockSpec((tm, d), lambda i: (i, 0)), _const_spec((1, d)), _const_spec((d, n))]
    out_shape = [jax.ShapeDtypeStruct((s, n_i), F32) for n_i in segs]
    out_specs = [pl.BlockSpec((tm, n_i), lambda i: (i, 0)) for n_i in segs]
    args = [x, g.reshape(1, d), w]
    if n_t:
        in_specs.append(_const_spec((n_t, d)))
        out_shape.append(jax.ShapeDtypeStruct((n_t, s), BF16))
        out_specs.append(pl.BlockSpec((n_t, tm), lambda i: (0, i)))
        args.append(wt)
    return pl.pallas_call(
        functools.partial(_norm_proj_kernel, segs=segs, n_t=n_t),
        out_shape=out_shape,
        grid=(s // tm,),
        in_specs=in_specs,
        out_specs=out_specs,
        compiler_params=_cparams(("parallel",)),
        name="norm_proj",
    )(*args)


def _t5_buckets():
    qi = np.arange(SWA_BLOCK)[:, None]
    kj = np.arange(2 * SWA_BLOCK)[None, :]
    dist = SWA_BLOCK + qi - kj
    valid = (dist >= 0) & (dist < SWA_BLOCK)
    max_exact = REL_BUCKETS // 2
    d0 = np.maximum(dist, 0)
    dl = np.maximum(dist, 1).astype(np.float32)
    large = max_exact + (np.log(dl / max_exact) / np.log(REL_MAX_DIST / max_exact)
                         * (REL_BUCKETS - max_exact)).astype(np.int32)
    large = np.minimum(large, REL_BUCKETS - 1)
    bucket = np.where(d0 < max_exact, d0, large).astype(np.int32)
    return bucket, valid.astype(np.int32)


def _bias_kernel(rb_ref, bucket_ref, valid_ref, o_ref):
    b = bucket_ref[...]
    ok = valid_ref[...] > 0
    for h in range(SWA_HEADS):
        acc = jnp.zeros(b.shape, F32)
        for k in range(REL_BUCKETS):
            acc = jnp.where(b == k, rb_ref[k, h], acc)
        o_ref[h] = jnp.where(ok, acc, NEG_INF)


def _bias_table(rel_bias):
    bucket, valid = _t5_buckets()
    tab = pl.pallas_call(
        _bias_kernel,
        out_shape=jax.ShapeDtypeStruct((SWA_HEADS, SWA_BLOCK, 2 * SWA_BLOCK), F32),
        in_specs=[pl.BlockSpec(memory_space=pltpu.SMEM),
                  pl.BlockSpec(memory_space=pltpu.VMEM), pl.BlockSpec(memory_space=pltpu.VMEM)],
        out_specs=pl.BlockSpec(memory_space=pltpu.VMEM),
        name="t5_bias_table",
    )(rel_bias, jnp.asarray(bucket), jnp.asarray(valid))
    grp = SWA_HEADS // SWA_KV_HEADS
    tab = tab.reshape(SWA_KV_HEADS, grp, SWA_BLOCK, 2 * SWA_BLOCK).transpose(0, 3, 1, 2)
    return tab.reshape(SWA_KV_HEADS, 2 * SWA_BLOCK, grp * SWA_BLOCK)


SWA_QB = 4


def _swa_kernel(sink_ref, cur_ref, prev_ref, vt_cur_ref, vt_prev_ref, qg_ref, kg_ref, bias_ref, o_ref, ot_scr):
    i = pl.program_id(0)
    blk, dh = SWA_BLOCK, SWA_HEAD_DIM
    grp = SWA_HEADS // SWA_KV_HEADS
    scale = dh ** -0.5

    qn = _head_rms_mxu(cur_ref[:, 0:512], qg_ref[...], dh) * scale
    k_all = jnp.concatenate([prev_ref[:, 512:640], cur_ref[:, 512:640]], axis=0)
    kn = _head_rms_mxu(k_all, kg_ref[...], dh)
    vt_all = jnp.concatenate([vt_prev_ref[...], vt_cur_ref[...]], axis=1)
    lane_k = lax.broadcasted_iota(jnp.int32, kn.shape, 1)
    lane_q = lax.broadcasted_iota(jnp.int32, (blk, 128), 1)
    key_row = lax.broadcasted_iota(jnp.int32, (2 * blk, grp * blk), 0)
    for kh in range(SWA_KV_HEADS):
        k_one = jnp.where(lane_k // dh == kh, kn, 0.0)
        k_dup = (k_one + pltpu.roll(k_one, dh, 1)).astype(BF16)
        sink = jnp.concatenate([jnp.full((1, blk), sink_ref[kh * grp + g], F32) for g in range(grp)], axis=1)
        for j in range(SWA_QB):
            rows = slice(j * blk, (j + 1) * blk)
            qs = []
            for g in range(grp):
                h = kh * grp + g
                q_pair = qn[rows, (h // 2) * 128:(h // 2 + 1) * 128]
                qs.append(jnp.where(lane_q // dh == h % 2, q_pair, 0.0))
            q_st = jnp.concatenate(qs, axis=0).astype(BF16)
            st = _dot_nt(k_dup[j * blk:(j + 2) * blk], q_st) + bias_ref[kh]
            if j == 0:
                st = jnp.where(jnp.logical_and(i == 0, key_row < blk), NEG_INF, st)
            m = jnp.maximum(jnp.max(st, axis=0, keepdims=True), sink)
            p = jnp.exp(st - m)
            denom = jnp.sum(p, axis=0, keepdims=True) + jnp.exp(sink - m)
            ot = _dot(vt_all[kh * dh:(kh + 1) * dh, j * blk:(j + 2) * blk], p.astype(BF16)) / denom
            for g in range(grp):
                h = kh * grp + g
                ot_scr[h * dh:(h + 1) * dh, rows] = ot[:, g * blk:(g + 1) * blk]
    o_ref[...] = ot_scr[...].T


def _swa(pa, vt, qn_g, kn_g, sinks, bias_t):
    s = pa.shape[0]
    tq = SWA_QB * SWA_BLOCK
    grp = SWA_HEADS // SWA_KV_HEADS
    prev = lambda i, sk: (jnp.maximum(i * SWA_QB - 1, 0), 0)
    return pl.pallas_call(
        _swa_kernel,
        out_shape=jax.ShapeDtypeStruct((s, GROUP_WIDTH), F32),
        grid_spec=pltpu.PrefetchScalarGridSpec(
            num_scalar_prefetch=1,
            grid=(s // tq,),
            in_specs=[pl.BlockSpec((tq, PA_W), lambda i, sk: (i, 0)),
                      pl.BlockSpec((SWA_BLOCK, PA_W), prev),
                      pl.BlockSpec((128, tq), lambda i, sk: (0, i)),
                      pl.BlockSpec((128, SWA_BLOCK), lambda i, sk: (0, jnp.maximum(i * SWA_QB - 1, 0))),
                      _const_spec((1, GROUP_WIDTH)), _const_spec((1, 128)),
                      _const_spec((SWA_KV_HEADS, 2 * SWA_BLOCK, grp * SWA_BLOCK))],
            out_specs=pl.BlockSpec((tq, GROUP_WIDTH), lambda i, sk: (i, 0)),
            scratch_shapes=[pltpu.VMEM((GROUP_WIDTH, tq), F32)],
        ),
        compiler_params=_cparams(("parallel",)),
        name="swa_attention",
    )(sinks, pa, pa, vt, vt, jnp.tile(qn_g, SWA_HEADS).reshape(1, -1), jnp.tile(kn_g, SWA_KV_HEADS).reshape(1, -1),
      bias_t)


CONV_TILE = 256
CONV_HALO = 32


def _conv_kernel(cur_ref, prev_ref, dww_ref, dwb_ref, lng_ref, lnb_ref, pww_ref, pwb_ref, o_ref, z_scr, zph_scr):
    i = pl.program_id(0)
    t, c = CONV_TILE, GROUP_WIDTH
    z_scr[CONV_HALO:CONV_HALO + t, :] = cur_ref[:, 0:c] * jax.nn.sigmoid(cur_ref[:, c:2 * c])
    zp = prev_ref[:, 0:c] * jax.nn.sigmoid(prev_ref[:, c:2 * c])
    z_scr[0:CONV_HALO, :] = jnp.where(i > 0, zp, 0.0)
    base = CONV_HALO - (CONV_WIDTH - 1)
    span = t + CONV_HALO - 8
    for ph in range(1, 8):
        zph_scr[ph - 1] = z_scr[ph:ph + span, :]
    acc = jnp.zeros((t, c), F32)
    for k in range(CONV_WIDTH):
        ph, off = (base + k) % 8, (base + k) // 8 * 8
        tap = z_scr[off:off + t, :] if ph == 0 else zph_scr[ph - 1, off:off + t, :]
        acc = acc + dww_ref[k:k + 1, :] * tap
    acc = acc + dwb_ref[...]
    xc = acc - jnp.mean(acc, axis=-1, keepdims=True)
    y = xc * lax.rsqrt(jnp.mean(xc * xc, axis=-1, keepdims=True) + NORM_EPS) * lng_ref[...] + lnb_ref[...]
    y = y * jax.nn.sigmoid(y)
    o_ref[...] = _dot(y.astype(BF16), pww_ref[...]) + pwb_ref[...]


def _conv(pb, dw_w, dw_b, ln_g, ln_b, pw_w, pw_b):
    s = pb.shape[0]
    c = GROUP_WIDTH
    r = CONV_TILE // CONV_HALO
    row = lambda v: v.reshape(1, c)
    return pl.pallas_call(
        _conv_kernel,
        out_shape=jax.ShapeDtypeStruct((s, c), F32),
        grid=(s // CONV_TILE,),
        in_specs=[pl.BlockSpec((CONV_TILE, PB_W), lambda i: (i, 0)),
                  pl.BlockSpec((CONV_HALO, PB_W), lambda i: (jnp.maximum(i * r - 1, 0), 0)),
                  _const_spec((CONV_WIDTH, c)), _const_spec((1, c)), _const_spec((1, c)), _const_spec((1, c)),
                  _const_spec((c, c)), _const_spec((1, c))],
        out_specs=pl.BlockSpec((CONV_TILE, c), lambda i: (i, 0)),
        scratch_shapes=[pltpu.VMEM((CONV_TILE + CONV_HALO, c), F32),
                        pltpu.VMEM((7, CONV_TILE + CONV_HALO - 8, c), F32)],
        compiler_params=_cparams(("parallel",)),
        name="conformer_conv",
    )(pb, pb, dw_w, row(dw_b), row(ln_g), row(ln_b), pw_w, row(pw_b))


MLA_TILE = 256


def _rot_half64(x):
    n = x.shape[-1]
    ax = x.ndim - 1
    lane = lax.broadcasted_iota(jnp.int32, x.shape, ax)
    return jnp.where((lane % 64) < 32, -pltpu.roll(x, n - 32, ax), pltpu.roll(x, 32, ax))


def _seg_rms64(x):
    n = x.shape[-1]
    lane = lax.broadcasted_iota(jnp.int32, x.shape, x.ndim - 1)
    x2 = x * x
    r = jnp.zeros(x.shape, F32)
    for h in range(n // 64):
        msk = jnp.logical_and(lane >= h * 64, lane < (h + 1) * 64)
        sm = jnp.sum(jnp.where(msk, x2, 0.0), axis=-1, keepdims=True)
        r = jnp.where(msk, lax.rsqrt(sm * (1.0 / 64) + NORM_EPS), r)
    return x * r


def _mla_proj_kernel(pc_ref, cos_ref, sin_ref, qag_ref, kvag_ref, wq_ref, wkv_ref,
                     qng_ref, qrg_ref, kng_ref, krg_ref, q_ref, k_ref, v_ref):
    h_n, dn = MLA_HEADS, MLA_NOPE
    scale = (MLA_NOPE + MLA_ROPE) ** -0.5 * LOG2E
    cos4 = cos_ref[...]
    sin4 = sin_ref[...]
    qn = _rms(pc_ref[:, 0:512], qag_ref[...]).astype(BF16)
    q = _dot(qn, wq_ref[...])
    kvn = _rms(pc_ref[:, 512:768], kvag_ref[...]).astype(BF16)
    kv = _dot(kvn, wkv_ref[...])
    qr = _seg_rms64(q[:, h_n * dn:]) * qrg_ref[...]
    qr = (qr * cos4 + _rot_half64(qr) * sin4) * scale
    kr = pc_ref[:, 768:896]
    kr = kr * lax.rsqrt(jnp.sum(kr * kr, axis=-1, keepdims=True) * (1.0 / MLA_ROPE) + NORM_EPS) * krg_ref[...]
    kr = (kr * cos4[:, 0:128] + _rot_half64(kr) * sin4[:, 0:128]).astype(BF16)
    lane = lax.broadcasted_iota(jnp.int32, (q.shape[0], 128), 1)
    for h in range(h_n):
        qnope = _rms(q[:, h * dn:(h + 1) * dn], qng_ref[...]) * scale
        blk = qr[:, (h // 2) * 128:(h // 2 + 1) * 128]
        if h % 2 == 1:
            blk = pltpu.roll(blk, 64, 1)
        blk = jnp.where(lane < 64, blk, 0.0)
        q_ref[h] = jnp.concatenate([qnope, blk], axis=-1).astype(BF16)
        knope = _rms(kv[:, h * dn:(h + 1) * dn], kng_ref[...]).astype(BF16)
        k_ref[h] = jnp.concatenate([knope, kr], axis=-1)
        v_ref[h] = kv[:, (h_n + h) * dn:(h_n + h + 1) * dn].astype(BF16)


def _mla_proj(pc, cos4, sin4, qa_g, kva_g, wq, wkv, qn_nope_g, qn_rope_g4, kn_nope_g, kn_rope_g_pad):
    s = pc.shape[0]
    t = MLA_TILE
    h = MLA_HEADS
    row = lambda v: v.reshape(1, -1)
    return pl.pallas_call(
        _mla_proj_kernel,
        out_shape=[jax.ShapeDtypeStruct((h, s, MLA_QK_PAD), BF16),
                   jax.ShapeDtypeStruct((h, s, MLA_QK_PAD), BF16),
                   jax.ShapeDtypeStruct((h, s, MLA_V), BF16)],
        grid=(s // t,),
        in_specs=[pl.BlockSpec((t, PC_W), lambda i: (i, 0)),
                  pl.BlockSpec((t, 256), lambda i: (i, 0)), pl.BlockSpec((t, 256), lambda i: (i, 0)),
                  _const_spec((1, 512)), _const_spec((1, 256)),
                  _const_spec((512, 768)), _const_spec((256, 1024)),
                  _const_spec((1, 128)), _const_spec((1, 256)), _const_spec((1, 128)), _const_spec((1, 128))],
        out_specs=[pl.BlockSpec((h, t, MLA_QK_PAD), lambda i: (0, i, 0)),
                   pl.BlockSpec((h, t, MLA_QK_PAD), lambda i: (0, i, 0)),
                   pl.BlockSpec((h, t, MLA_V), lambda i: (0, i, 0))],
        compiler_params=_cparams(("parallel",)),
        name="mla_proj",
    )(pc, cos4, sin4, row(qa_g), row(kva_g), wq, wkv, row(qn_nope_g), row(qn_rope_g4), row(kn_nope_g),
      row(kn_rope_g_pad))


FLASH_BQ = 2048
FLASH_BK = 512
FLASH_SUM_ROWS = 16


def _flash_kernel(q_ref, k_ref, vt_ref, o_ref, sa_scr, sb_scr, m_scr, acc_scr):
    bq, bk = FLASH_BQ, FLASH_BK
    per_q = bq // bk
    assert per_q % 2 == 0
    qi = pl.program_id(1)
    m_scr[...] = jnp.full(m_scr.shape, NEG_INF, F32)
    acc_scr[...] = jnp.zeros(acc_scr.shape, F32)
    ones_rows = jnp.ones((FLASH_SUM_ROWS, bk), BF16)

    def scores(j, s_scr, c0=0):
        k = k_ref[pl.ds(pl.multiple_of(j * bk, bk), bk), :]
        s_scr[:, c0:] = _dot_nt(k, q_ref[c0:, :])

    def update(j, s_scr, masked, c0=0, c1=bq):
        st = s_scr[:, c0:c1]
        if masked:
            kpos = j * bk + lax.broadcasted_iota(jnp.int32, st.shape, 0)
            qpos = qi * bq + c0 + lax.broadcasted_iota(jnp.int32, st.shape, 1)
            st = jnp.where(kpos <= qpos, st, NEG_INF)
        m_prev = m_scr[:, c0:c1]
        m_new = jnp.maximum(m_prev, jnp.max(st, axis=0, keepdims=True))
        alpha = jnp.exp2(m_prev - m_new)
        p = jnp.exp2(st - m_new)
        vt_ext = jnp.concatenate([vt_ref[j], ones_rows], axis=0)
        acc_scr[:, c0:c1] = alpha * acc_scr[:, c0:c1] + _dot(vt_ext, p.astype(BF16))
        m_scr[:, c0:c1] = m_new

    scores(0, sa_scr)

    def body(i, carry):
        for c in range(0, per_q, 2):
            scores(per_q * i + c + 1, sb_scr)
            update(per_q * i + c, sa_scr, False)
            scores(per_q * i + c + 2, sa_scr)
            update(per_q * i + c + 1, sb_scr, False)
        return carry

    lax.fori_loop(0, qi, body, 0)
    first = qi * per_q
    bufs = (sa_scr, sb_scr)
    for c in range(per_q):
        if c + 1 < per_q:
            scores(first + c + 1, bufs[(c + 1) % 2], (c + 1) * bk)
        update(first + c, bufs[c % 2], True, c * bk, (c + 1) * bk)
        if c + 1 < per_q:
            update(first + c, bufs[c % 2], False, (c + 1) * bk, bq)
    o_ref[...] = (acc_scr[0:MLA_V, :] / acc_scr[MLA_V:MLA_V + 1, :]).T


def _flash(q, k, vt):
    h, s, _ = q.shape
    nkb = s // FLASH_BK
    return pl.pallas_call(
        _flash_kernel,
        out_shape=jax.ShapeDtypeStruct((s, h * MLA_V), F32),
        grid=(h, s // FLASH_BQ),
        in_specs=[pl.BlockSpec((None, FLASH_BQ, MLA_QK_PAD), lambda hh, i: (hh, i, 0)),
                  pl.BlockSpec((None, s, MLA_QK_PAD), lambda hh, i: (hh, 0, 0)),
                  pl.BlockSpec((None, nkb, MLA_V, FLASH_BK), lambda hh, i: (hh, 0, 0, 0))],
        out_specs=pl.BlockSpec((FLASH_BQ, MLA_V), lambda hh, i: (i, hh)),
        scratch_shapes=[pltpu.VMEM((FLASH_BK, FLASH_BQ), F32), pltpu.VMEM((FLASH_BK, FLASH_BQ), F32),
                        pltpu.VMEM((1, FLASH_BQ), F32), pltpu.VMEM((MLA_V + FLASH_SUM_ROWS, FLASH_BQ), F32)],
        compiler_params=_cparams(("parallel", "arbitrary"), vmem=56 * 1024 * 1024),
        name="mla_flash",
    )(q, k, vt)


def _split3_dot(a, sel):
    a1 = a.astype(BF16)
    r1 = a - a1.astype(F32)
    a2 = r1.astype(BF16)
    a3 = (r1 - a2.astype(F32)).astype(BF16)
    return _dot(a1, sel) + _dot(a2, sel) + _dot(a3, sel)


def _hi_dot(a, b):
    return lax.dot_general(a, b, (((1,), (0,)), ((), ())), precision=lax.Precision.HIGHEST,
                           preferred_element_type=F32)


def _ssm_kernel(u_ref, arc_ref, aic_ref, arr_ref, air_ref, ldt_ref, bre_ref, bim_ref, btre_ref, btim_ref,
                ctre_ref, ctim_ref, dcol_ref, y_ref, toep_scr, *, nch):
    tc, n_in, n_st = SSM_CHUNK, SSM_GROUP_IN, SSM_STATE
    w = tc * n_in
    dt = jnp.exp(ldt_ref[...])
    ar_c, ai_c = arc_ref[...], aic_ref[...]
    ar_r, ai_r = arr_ref[...], air_ref[...]

    def lam_pow(ar, ai, e):
        mag = jnp.exp(ar * dt * e)
        ang = ai * dt * e
        return mag * jnp.cos(ang), mag * jnp.sin(ang)

    def zoh_coef(ar, ai):
        lr, li = lam_pow(ar, ai, 1.0)
        x, y = lr - 1.0, li
        den = ar * ar + ai * ai
        return (x * ar + y * ai) / den, (y * ar - x * ai) / den

    lane_w = lax.broadcasted_iota(jnp.int32, (tc, w), 1)
    sel_tau = (lane_w // n_in == lax.broadcasted_iota(jnp.int32, (tc, w), 0)).astype(BF16)
    lane_o = lax.broadcasted_iota(jnp.int32, (n_in, w), 1)
    sel_o = (lane_o % n_in == lax.broadcasted_iota(jnp.int32, (n_in, w), 0)).astype(BF16)

    tau = lax.broadcasted_iota(jnp.int32, (n_st, tc), 1).astype(F32)
    lre, lim = lam_pow(ar_c, ai_c, tau)
    lre_x, lim_x = _split3_dot(lre, sel_tau), _split3_dot(lim, sel_tau)
    rre, rim = lam_pow(ar_c, ai_c, (tc - 1.0) - tau)
    rre_x, rim_x = _split3_dot(rre, sel_tau), _split3_dot(rim, sel_tau)
    ctre_x, ctim_x = _split3_dot(ctre_ref[...], sel_o), _split3_dot(ctim_ref[...], sel_o)

    e_re = lre_x * ctre_x - lim_x * ctim_x
    e_im = lre_x * ctim_x + lim_x * ctre_x

    cr_c, ci_c = zoh_coef(ar_c, ai_c)
    bb_re = cr_c * bre_ref[...] - ci_c * bim_ref[...]
    bb_im = cr_c * bim_ref[...] + ci_c * bre_ref[...]
    bb_re_x, bb_im_x = _split3_dot(bb_re, sel_o), _split3_dot(bb_im, sel_o)
    f_re = (rre_x * bb_re_x - rim_x * bb_im_x).astype(BF16)
    f_im = (rre_x * bb_im_x + rim_x * bb_re_x).astype(BF16)

    cr_r, ci_r = zoh_coef(ar_r, ai_r)
    bbt_re = cr_r * btre_ref[...] - ci_r * btim_ref[...]
    bbt_im = cr_r * btim_ref[...] + ci_r * btre_ref[...]
    g_row = _hi_dot(bbt_re, e_re) - _hi_dot(bbt_im, e_im)
    lane_g = lax.broadcasted_iota(jnp.int32, (n_in, w), 1)
    row_g = lax.broadcasted_iota(jnp.int32, (n_in, w), 0)
    g_row = g_row + jnp.where(lane_g == row_g, dcol_ref[...], 0.0)
    for s in range(tc):
        rolled = g_row if s == 0 else pltpu.roll(g_row, n_in * s, 1)
        toep_scr[s * n_in:(s + 1) * n_in, :] = jnp.where(lane_g >= n_in * s, rolled, 0.0).astype(BF16)

    u = u_ref[...]
    s_re = _dot_nt(u, f_re)
    s_im = _dot_nt(u, f_im)
    row = lax.broadcasted_iota(jnp.int32, (nch, n_st), 0)
    sh = 1
    while sh < nch:
        pr, pi = lam_pow(ar_r, ai_r, float(tc * sh))
        re_s = jnp.where(row >= sh, pltpu.roll(s_re, sh, 0), 0.0)
        im_s = jnp.where(row >= sh, pltpu.roll(s_im, sh, 0), 0.0)
        s_re, s_im = s_re + pr * re_s - pi * im_s, s_im + pr * im_s + pi * re_s
        sh *= 2
    p_re = jnp.where(row >= 1, pltpu.roll(s_re, 1, 0), 0.0)
    p_im = jnp.where(row >= 1, pltpu.roll(s_im, 1, 0), 0.0)
    l1r, l1i = lam_pow(ar_c, ai_c, 1.0)
    h_re = (l1r * e_re - l1i * e_im).astype(BF16)
    h_im = (l1r * e_im + l1i * e_re).astype(BF16)
    y = _dot(u, toep_scr[...]) + _dot(p_re.astype(BF16), h_re) - _dot(p_im.astype(BF16), h_im)
    y_ref[...] = y


def _ssm(u_r, a_re, a_im, log_dt, b_re, b_im, c_re, c_im, d_skip):
    g, nch, w = u_r.shape
    n_st, n_in = SSM_STATE, SSM_GROUP_IN
    col = lambda v: v.reshape(g, n_st, 1)
    rw = lambda v: v.reshape(g, 1, n_st)
    spec = lambda *shape: pl.BlockSpec((None,) + shape, lambda i: (i,) + (0,) * len(shape))
    return pl.pallas_call(
        functools.partial(_ssm_kernel, nch=nch),
        out_shape=jax.ShapeDtypeStruct((g, nch, w), F32),
        grid=(g,),
        in_specs=[spec(nch, w), spec(n_st, 1), spec(n_st, 1), spec(1, n_st), spec(1, n_st), spec(1, 1),
                  spec(n_st, n_in), spec(n_st, n_in), spec(n_in, n_st), spec(n_in, n_st),
                  spec(n_st, n_in), spec(n_st, n_in), spec(n_in, 1)],
        out_specs=spec(nch, w),
        scratch_shapes=[pltpu.VMEM((w, w), BF16)],
        compiler_params=_cparams(("parallel",)),
        name="s5_ssm",
    )(u_r, col(a_re), col(a_im), rw(a_re), rw(a_im), log_dt.reshape(g, 1, 1), b_re, b_im,
      jnp.swapaxes(b_re, 1, 2), jnp.swapaxes(b_im, 1, 2), jnp.swapaxes(c_re, 1, 2), jnp.swapaxes(c_im, 1, 2),
      d_skip.reshape(g, n_in, 1))


OUT_TILE = 256


def _mix_out_kernel(x_ref, oa_ref, ob_ref, oc_ref, yd_ref, gw_ref, gb_ref, mg_ref, wo_ref, o_ref):
    c = GROUP_WIDTH
    y = jax.nn.gelu(yd_ref[...])
    od = y * jax.nn.sigmoid(_dot(y.astype(BF16), gw_ref[...]) + gb_ref[...])
    acc = x_ref[...]
    for m, o in enumerate((oa_ref[...], ob_ref[...], oc_ref[...], od)):
        n = _rms(o, mg_ref[:, m * c:(m + 1) * c]).astype(BF16)
        acc = acc + _dot(n, wo_ref[m * c:(m + 1) * c, :])
    o_ref[...] = acc


def _mix_out(x, oa, ob, oc, yd, glu_w, glu_b, mix_g, w_out):
    s, d = x.shape
    c = GROUP_WIDTH
    t = OUT_TILE
    tile = lambda n: pl.BlockSpec((t, n), lambda i: (i, 0))
    return pl.pallas_call(
        _mix_out_kernel,
        out_shape=jax.ShapeDtypeStruct((s, d), F32),
        grid=(s // t,),
        in_specs=[tile(d), tile(c), tile(c), tile(c), tile(c),
                  _const_spec((c, c)), _const_spec((1, c)), _const_spec((1, d)), _const_spec((d, d))],
        out_specs=tile(d),
        compiler_params=_cparams(("parallel",)),
        name="mix_out",
    )(x, oa, ob, oc, yd, glu_w, glu_b.reshape(1, c), mix_g.reshape(1, d), w_out)


XA_TILE = 256


def _xattn_kernel(x_ref, lg_ref, wq_ref, qg_ref, k_ref, vt_ref, kg_ref, wo_ref, o_ref):
    dh = XA_HEAD_DIM
    x = x_ref[...]
    q = _dot(_rms(x, lg_ref[...]).astype(BF16), wq_ref[...])
    qn = (_head_rms_mxu(q, qg_ref[...], dh) * dh ** -0.5).astype(BF16)
    kn = _head_rms_mxu(k_ref[...], kg_ref[...], dh).astype(BF16)
    outs = []
    for h in range(XA_HEADS):
        cols = slice(h * dh, (h + 1) * dh)
        st = _dot_nt(kn[:, cols], qn[:, cols])
        p = jnp.exp(st - jnp.max(st, axis=0, keepdims=True))
        outs.append(_dot(vt_ref[cols, :], p.astype(BF16)) / jnp.sum(p, axis=0, keepdims=True))
    o = jnp.concatenate(outs, axis=0).T
    o_ref[...] = x + _dot(o.astype(BF16), wo_ref[...])


def _xattn(x, ln_g, wq, qn_g, k_mem, vt_mem, kn_g, wo):
    s, d = x.shape
    m, hw = k_mem.shape
    t = XA_TILE
    return pl.pallas_call(
        _xattn_kernel,
        out_shape=jax.ShapeDtypeStruct((s, d), F32),
        grid=(s // t,),
        in_specs=[pl.BlockSpec((t, d), lambda i: (i, 0)), _const_spec((1, d)), _const_spec((d, hw)),
                  _const_spec((1, hw)), _const_spec((m, hw)), _const_spec((hw, m)), _const_spec((1, hw)),
                  _const_spec((hw, d))],
        out_specs=pl.BlockSpec((t, d), lambda i: (i, 0)),
        compiler_params=_cparams(("parallel",)),
        name="mem_xattn",
    )(x, ln_g.reshape(1, d), wq, jnp.tile(qn_g, XA_HEADS).reshape(1, -1), k_mem, vt_mem,
      jnp.tile(kn_g, XA_HEADS).reshape(1, -1), wo)


ROUTER_TILE = 256
ROUTER_LANES = 128


def _router_kernel(x_ref, lg_ref, whi_ref, wlo_ref, b_ref, h_ref, r_ref, cnt_ref, carry_scr):
    i = pl.program_id(0)
    t = ROUTER_TILE
    ng, epg = N_EXPERT_GROUPS, EXPERTS_PER_GROUP

    @pl.when(i == 0)
    def _():
        carry_scr[...] = jnp.zeros(carry_scr.shape, F32)

    h = _rms(x_ref[...], lg_ref[...])
    h_hi = h.astype(BF16)
    h_lo = (h - h_hi.astype(F32)).astype(BF16)
    _rows_store(h_ref, 0, t, _pack_bf16_pair(h[:, 0:D_MODEL // 2], h[:, D_MODEL // 2:]))
    logits = _dot(h_hi, whi_ref[...]) + _dot(h_hi, wlo_ref[...]) + _dot(h_lo, whi_ref[...]) + b_ref[...]
    lane = lax.broadcasted_iota(jnp.int32, (t, ROUTER_LANES), 1)
    big = jnp.int32(ROUTER_LANES)

    def first_max(vals, mask):
        v = jnp.where(mask, vals, -jnp.inf)
        mx = jnp.max(v, axis=-1, keepdims=True)
        idx = jnp.min(jnp.where(jnp.logical_and(mask, v == mx), lane, big), axis=-1, keepdims=True)
        return mx, idx

    gmask = lane < ng
    gmax, gidx = first_max(logits, gmask)
    g_w = 1.0 / jnp.sum(jnp.where(gmask, jnp.exp(logits - gmax), 0.0), axis=-1, keepdims=True)
    emask = jnp.logical_and(lane >= ng + gidx * epg, lane < ng + (gidx + 1) * epg)
    v1, i1 = first_max(logits, emask)
    v2, i2 = first_max(logits, jnp.logical_and(emask, lane != i1))
    e2 = jnp.exp(v2 - v1)
    gate1 = g_w / (1.0 + e2)
    gate2 = g_w * e2 / (1.0 + e2)

    oh1 = lane == i1
    oh2 = lane == i2
    both = (oh1.astype(F32) + oh2.astype(F32))
    tri = (lax.broadcasted_iota(jnp.int32, (t, t), 1) < lax.broadcasted_iota(jnp.int32, (t, t), 0)).astype(BF16)
    before = _dot(tri, both.astype(BF16)) + carry_scr[...]
    rank1 = jnp.sum(jnp.where(oh1, before, 0.0), axis=-1, keepdims=True)
    rank2 = jnp.sum(jnp.where(oh2, before, 0.0), axis=-1, keepdims=True)
    carry_scr[...] = carry_scr[...] + jnp.sum(both, axis=0, keepdims=True)
    cnt_ref[...] = jnp.broadcast_to(carry_scr[...], cnt_ref.shape)

    out = jnp.zeros((t, ROUTER_LANES), F32)
    for k, val in enumerate(((i1 - ng).astype(F32), (i2 - ng).astype(F32), gate1, gate2, rank1, rank2)):
        out = jnp.where(lane == k, val, out)
    r_ref[...] = out


def _router(x, ln_g, w_hi, w_lo, bias):
    s, d = x.shape
    t = ROUTER_TILE
    return pl.pallas_call(
        _router_kernel,
        out_shape=[jax.ShapeDtypeStruct((s * ROW_SUB, 128), jnp.uint32), jax.ShapeDtypeStruct((s, ROUTER_LANES), F32),
                   jax.ShapeDtypeStruct((8, ROUTER_LANES), F32)],
        grid=(s // t,),
        in_specs=[pl.BlockSpec((t, d), lambda i: (i, 0)), _const_spec((1, d)),
                  _const_spec((d, ROUTER_LANES)), _const_spec((d, ROUTER_LANES)), _const_spec((1, ROUTER_LANES))],
        out_specs=[pl.BlockSpec((t * ROW_SUB, 128), lambda i: (i, 0)), pl.BlockSpec((t, ROUTER_LANES), lambda i: (i, 0)),
                   pl.BlockSpec((8, ROUTER_LANES), lambda i: (0, 0))],
        scratch_shapes=[pltpu.VMEM((1, ROUTER_LANES), F32)],
        compiler_params=_cparams(("arbitrary",)),
        name="moe_router",
    )(x, ln_g.reshape(1, d), w_hi, w_lo, bias)


def _ffn_kernel(be_ref, nu_ref, nx_ref, x_ref, wg_hbm, wu_hbm, wd_hbm, o_ref,
                wg_st, wu_st, wd_st, wg_scr, wu_scr, wd_scr, sem, *, layer):
    b = pl.program_id(0)
    half = D_MODEL // 2
    e = be_ref[b]
    prev = be_ref[jnp.maximum(b - 1, 0)]

    def fetch(expert):
        return (pltpu.make_async_copy(wg_hbm.at[layer, expert], wg_st, sem.at[0]),
                pltpu.make_async_copy(wu_hbm.at[layer, expert], wu_st, sem.at[1]),
                pltpu.make_async_copy(wd_hbm.at[layer, expert], wd_st, sem.at[2]))

    @pl.when(b == 0)
    def _():
        for c in fetch(e):
            c.start()

    @pl.when(jnp.logical_or(b == 0, e != prev))
    def _():
        for c in fetch(e):
            c.wait()
        wg_scr[...] = wg_st[...].astype(BF16)
        wu_scr[...] = wu_st[...].astype(BF16)
        wd_scr[...] = wd_st[...].astype(BF16)
        nxt = nx_ref[e]

        @pl.when(nxt != e)
        def _():
            for c in fetch(nxt):
                c.start()

    @pl.when(b < nu_ref[0])
    def _():
        xa, xb = _unpack_bf16_pair(_rows_load(x_ref, 0, EXPERT_BLOCK))
        xa, xb = xa.astype(BF16), xb.astype(BF16)
        g = _dot(xa, wg_scr[0:half, :]) + _dot(xb, wg_scr[half:, :])
        u = _dot(xa, wu_scr[0:half, :]) + _dot(xb, wu_scr[half:, :])
        hid = ((g * jax.nn.sigmoid(g)) * u).astype(BF16)
        _rows_store(o_ref, 0, EXPERT_BLOCK, _pack_bf16_pair(_dot(hid, wd_scr[:, 0:half]), _dot(hid, wd_scr[:, half:])))

    @pl.when(b >= nu_ref[0])
    def _():
        o_ref[...] = jnp.zeros(o_ref.shape, o_ref.dtype)


def _ffn(xs, block_expert, n_used, next_expert, wg, wu, wd, layer):
    d = D_MODEL
    nb = xs.shape[0] // (EXPERT_BLOCK * ROW_SUB)
    blk = (EXPERT_BLOCK * ROW_SUB, 128)
    ff = EXPERT_FF
    return pl.pallas_call(
        functools.partial(_ffn_kernel, layer=layer),
        out_shape=jax.ShapeDtypeStruct(xs.shape, jnp.uint32),
        grid_spec=pltpu.PrefetchScalarGridSpec(
            num_scalar_prefetch=3,
            grid=(nb,),
            in_specs=[pl.BlockSpec(blk, lambda b, be, nu, nx: (b, 0)),
                      pl.BlockSpec(memory_space=pl.ANY), pl.BlockSpec(memory_space=pl.ANY),
                      pl.BlockSpec(memory_space=pl.ANY)],
            out_specs=pl.BlockSpec(blk, lambda b, be, nu, nx: (b, 0)),
            scratch_shapes=[pltpu.VMEM((d, ff), F32), pltpu.VMEM((d, ff), F32), pltpu.VMEM((ff, d), F32),
                            pltpu.VMEM((d, ff), BF16), pltpu.VMEM((d, ff), BF16), pltpu.VMEM((ff, d), BF16),
                            pltpu.SemaphoreType.DMA((3,))],
        ),
        compiler_params=_cparams(("arbitrary",)),
        name="moe_ffn",
    )(block_expert, n_used, next_expert, xs, wg, wu, wd)


MOE_ROW_TILE = 256


ROW_SUB = 8


def _rows_store(ref, base, n, packed):
    for sub in range(ROW_SUB):
        ref[pl.ds(base * ROW_SUB + sub, n, stride=ROW_SUB), :] = packed[:, sub * 128:(sub + 1) * 128]


def _rows_load(ref, base, n):
    return jnp.concatenate([ref[pl.ds(base * ROW_SUB + sub, n, stride=ROW_SUB), :] for sub in range(ROW_SUB)], axis=1)


def _row_tile(idx):
    return pl.ds(pl.multiple_of(idx * ROW_SUB, ROW_SUB), ROW_SUB)


def _pack_bf16_pair(a, b):
    hi = lax.bitcast_convert_type(a.astype(BF16).astype(F32), jnp.uint32)
    lo = lax.bitcast_convert_type(b.astype(BF16).astype(F32), jnp.uint32)
    return hi | (lo >> 16)


def _unpack_bf16_pair(w):
    a = lax.bitcast_convert_type(w & jnp.uint32(0xFFFF0000), F32)
    b = lax.bitcast_convert_type(w << 16, F32)
    return a, b


def _dispatch_kernel(dest_ref, h_ref, init_hbm, xs_hbm, sem):
    del init_hbm
    t = MOE_ROW_TILE

    def row_copy(r, k):
        return pltpu.make_async_copy(h_ref.at[_row_tile(r)], xs_hbm.at[_row_tile(dest_ref[0, 0, 2 * r + k])], sem)

    def issue(r, carry):
        row_copy(r, 0).start(priority=0)
        row_copy(r, 1).start(priority=1)
        return carry

    def drain(r, carry):
        row_copy(r, 0).wait()
        row_copy(r, 1).wait()
        return carry

    lax.fori_loop(0, t, issue, 0, unroll=8)
    lax.fori_loop(0, t, drain, 0, unroll=8)


def _dispatch(h_packed, dest, n_rows):
    s = h_packed.shape[0] // ROW_SUB
    t = MOE_ROW_TILE
    return pl.pallas_call(
        _dispatch_kernel,
        out_shape=jax.ShapeDtypeStruct((n_rows * ROW_SUB, 128), jnp.uint32),
        grid=(s // t,),
        in_specs=[pl.BlockSpec((1, 1, 2 * t), lambda i: (i, 0, 0), memory_space=pltpu.SMEM),
                  pl.BlockSpec((t * ROW_SUB, 128), lambda i: (i, 0)), pl.BlockSpec(memory_space=pl.ANY)],
        out_specs=pl.BlockSpec(memory_space=pl.ANY),
        scratch_shapes=[pltpu.SemaphoreType.DMA(())],
        input_output_aliases={2: 0},
        compiler_params=_cparams(("arbitrary",)),
        name="moe_dispatch",
    )(dest.reshape(s // t, 1, 2 * t), h_packed, jnp.zeros((n_rows * ROW_SUB, 128), jnp.uint32))


def _combine_kernel(dest_ref, x_ref, route_ref, rows_hbm, o_ref, buf, sem):
    t = MOE_ROW_TILE
    half = D_MODEL // 2

    def row_copy(r, k):
        return pltpu.make_async_copy(rows_hbm.at[_row_tile(dest_ref[0, 0, 2 * r + k])], buf.at[_row_tile(k * t + r)], sem)

    def issue(r, carry):
        row_copy(r, 0).start(priority=0)
        row_copy(r, 1).start(priority=1)
        return carry

    def drain(r, carry):
        row_copy(r, 0).wait()
        row_copy(r, 1).wait()
        return carry

    lax.fori_loop(0, t, issue, 0, unroll=8)
    lax.fori_loop(0, t, drain, 0, unroll=8)
    g0 = route_ref[:, 2:3]
    g1 = route_ref[:, 3:4]
    a0, b0 = _unpack_bf16_pair(_rows_load(buf, 0, t))
    a1, b1 = _unpack_bf16_pair(_rows_load(buf, t, t))
    o_ref[:, 0:half] = x_ref[:, 0:half] + g0 * a0 + g1 * a1
    o_ref[:, half:] = x_ref[:, half:] + g0 * b0 + g1 * b1


def _combine(x, route, rows, dest):
    s, d = x.shape
    t = MOE_ROW_TILE
    return pl.pallas_call(
        _combine_kernel,
        out_shape=jax.ShapeDtypeStruct((s, d), F32),
        grid=(s // t,),
        in_specs=[pl.BlockSpec((1, 1, 2 * t), lambda i: (i, 0, 0), memory_space=pltpu.SMEM),
                  pl.BlockSpec((t, d), lambda i: (i, 0)), pl.BlockSpec((t, ROUTER_LANES), lambda i: (i, 0)),
                  pl.BlockSpec(memory_space=pl.ANY)],
        out_specs=pl.BlockSpec((t, d), lambda i: (i, 0)),
        scratch_shapes=[pltpu.VMEM((2 * t * ROW_SUB, 128), jnp.uint32), pltpu.SemaphoreType.DMA(())],
        compiler_params=_cparams(("arbitrary",)),
        name="moe_combine",
    )(dest.reshape(s // t, 1, 2 * t), x, route, rows)


def _rope_tables4(s):
    inv = 1.0 / (ROPE_THETA ** (jnp.arange(0, MLA_ROPE, 2, dtype=F32) / MLA_ROPE))
    ang = jnp.arange(s, dtype=F32)[:, None] * inv[None, :]
    cos, sin = jnp.cos(ang), jnp.sin(ang)
    return jnp.tile(cos, (1, 8)), jnp.tile(sin, (1, 8))


def _moe_layer(x, ln_g, w_group, b_group, w_expert, b_expert, wg, wu, wd, layer):
    s, d = x.shape
    ng, ne = N_EXPERT_GROUPS, N_EXPERTS
    w_r = jnp.zeros((d, ROUTER_LANES), F32).at[:, :ng].set(w_group).at[:, ng:ng + ne].set(w_expert)
    w_hi = w_r.astype(BF16)
    w_lo = (w_r - w_hi.astype(F32)).astype(BF16)
    b_r = jnp.zeros((1, ROUTER_LANES), F32).at[0, :ng].set(b_group).at[0, ng:ng + ne].set(b_expert)
    h_packed, route, cnt = _router(x, ln_g, w_hi, w_lo, b_r)
    e_id = route[:, 0:2].astype(jnp.int32)
    rank = route[:, 4:6].astype(jnp.int32)
    counts = cnt[0, ng:ng + ne].astype(jnp.int32)
    n_blocks = -(-(2 * s + ne * (EXPERT_BLOCK - 1)) // EXPERT_BLOCK)
    padded = (counts + EXPERT_BLOCK - 1) // EXPERT_BLOCK * EXPERT_BLOCK
    pad_ends = jnp.cumsum(padded)
    pad_starts = pad_ends - padded
    dest = pad_starts[e_id] + rank
    block_starts = jnp.arange(n_blocks, dtype=jnp.int32) * EXPERT_BLOCK
    block_expert = jnp.minimum(jnp.sum(pad_ends[None, :] <= block_starts[:, None], axis=1), ne - 1).astype(jnp.int32)
    n_used = (pad_ends[-1] // EXPERT_BLOCK).astype(jnp.int32).reshape(1)
    xs = _dispatch(h_packed, dest, n_blocks * EXPERT_BLOCK)
    experts = jnp.arange(ne, dtype=jnp.int32)
    present = jnp.any(block_expert[None, :] == experts[:, None], axis=1)
    later = jnp.where(jnp.logical_and(present[None, :], experts[None, :] > experts[:, None]), experts[None, :], ne)
    next_expert = jnp.where(jnp.min(later, axis=1) < ne, jnp.min(later, axis=1), experts).astype(jnp.int32)
    rows = _ffn(xs, block_expert, n_used, next_expert, wg, wu, wd, layer)
    return _combine(x, route, rows, dest)


def _perm_w_in(w_in):
    sp = np.cumsum((512, 128, 128, 512, 512, 512, 256, 64, 512))
    a_q, a_k, a_v, b_a, b_g, c_q, c_kv, c_kr, d_u = jnp.split(w_in, [int(v) for v in sp[:-1]], axis=-1)
    zero = jnp.zeros((w_in.shape[0], 64), w_in.dtype)
    return jnp.concatenate([a_q, a_k, a_v, b_a, b_g, c_q, c_kv, c_kr, zero, d_u], axis=-1)


def _perm_heads(w, parts):
    k = w.shape[0]
    wh = w.reshape(k, MLA_HEADS, sum(parts))
    offs = np.cumsum((0,) + tuple(parts))
    return jnp.concatenate([wh[:, :, offs[j]:offs[j + 1]].reshape(k, -1) for j in range(len(parts))], axis=-1)


def kernel(x, mem, rel_bias, ln_mix_g, w_in, swa_qn_g, swa_kn_g, swa_sinks, conv_dw_w, conv_dw_b, conv_ln_g, conv_ln_b, conv_pw_w, conv_pw_b, mla_qa_g, mla_kva_g, mla_wq_b, mla_wkv_b, mla_qn_nope_g, mla_qn_rope_g, mla_kn_nope_g, mla_kn_rope_g, ssm_a_re, ssm_a_im, ssm_log_dt, ssm_b_re, ssm_b_im, ssm_c_re, ssm_c_im, ssm_d, ssm_glu_w, ssm_glu_b, mix_out_g, w_out, ln_cross_g, mem_ln_g, xa_wq, xa_wk, xa_wv, xa_qn_g, xa_kn_g, xa_wo, ln_moe_g, moe_w_group, moe_b_group, moe_w_expert, moe_b_expert, moe_w_gate, moe_w_up, moe_w_down):
    b, s, d = x.shape
    assert b == 1 and d == D_MODEL and s % FLASH_BQ == 0
    xf = x.reshape(s, d)
    memf = mem.reshape(mem.shape[1], d)
    cos4, sin4 = _rope_tables4(s)
    bias_tab = _bias_table(rel_bias)
    nch = s // SSM_CHUNK
    g_n, n_in = SSM_GROUPS, SSM_GROUP_IN
    for l in range(DEPTH):
        pa, pb, pc, pd, vt_a = _norm_proj(xf, ln_mix_g[l], _perm_w_in(w_in[l]).astype(BF16), P_SEGS, 256,
                                          wt=w_in[l][:, 640:768].T.astype(BF16))
        o_a = _swa(pa, vt_a, swa_qn_g[l], swa_kn_g[l], swa_sinks[l], bias_tab)
        o_b = _conv(pb, conv_dw_w[l], conv_dw_b[l], conv_ln_g[l], conv_ln_b[l], conv_pw_w[l].astype(BF16),
                    conv_pw_b[l])
        q, k, v = _mla_proj(pc, cos4, sin4, mla_qa_g[l], mla_kva_g[l],
                            _perm_heads(mla_wq_b[l], (MLA_NOPE, MLA_ROPE)).astype(BF16),
                            _perm_heads(mla_wkv_b[l], (MLA_NOPE, MLA_V)).astype(BF16),
                            mla_qn_nope_g[l], jnp.tile(mla_qn_rope_g[l], MLA_HEADS), mla_kn_nope_g[l],
                            jnp.concatenate([mla_kn_rope_g[l], jnp.zeros((64,), F32)]))
        vt = v.reshape(MLA_HEADS, s // FLASH_BK, FLASH_BK, MLA_V).swapaxes(2, 3)
        o_c = _flash(q, k, vt)
        u_r = pd.astype(BF16).reshape(nch, SSM_CHUNK, g_n, n_in).transpose(2, 0, 1, 3).reshape(g_n, nch, -1)
        y_r = _ssm(u_r, ssm_a_re[l], ssm_a_im[l], ssm_log_dt[l], ssm_b_re[l], ssm_b_im[l], ssm_c_re[l],
                   ssm_c_im[l], ssm_d[l])
        y_d = y_r.reshape(g_n, nch, SSM_CHUNK, n_in).transpose(1, 2, 0, 3).reshape(s, GROUP_WIDTH)
        xf = _mix_out(xf, o_a, o_b, o_c, y_d, ssm_glu_w[l].astype(BF16), ssm_glu_b[l], mix_out_g[l],
                      w_out[l].astype(BF16))
        k_mem, vt_mem = _norm_proj(memf, mem_ln_g[l], xa_wk[l].astype(BF16), (XA_HEADS * XA_HEAD_DIM,), memf.shape[0],
                                   wt=xa_wv[l].T.astype(BF16))
        xf = _xattn(xf, ln_cross_g[l], xa_wq[l].astype(BF16), xa_qn_g[l], k_mem, vt_mem, xa_kn_g[l],
                    xa_wo[l].astype(BF16))
        xf = _moe_layer(xf, ln_moe_g[l], moe_w_group[l], moe_b_group[l], moe_w_expert[l], moe_b_expert[l],
                        moe_w_gate, moe_w_up, moe_w_down, l)
    return xf.reshape(b, s, d)
```

```python
import functools
import math

import jax
import jax.numpy as jnp
import numpy as np
from jax import lax
from jax.experimental import pallas as pl
from jax.experimental.pallas import tpu as pltpu

F32 = jnp.float32
BF16 = jnp.bfloat16

D_MODEL = 2048
DEPTH = 4
GROUP_WIDTH = 512
SWA_HEADS = 8
SWA_KV_HEADS = 2
SWA_HEAD_DIM = 64
SWA_BLOCK = 128
CONV_WIDTH = 31
MLA_HEADS = 4
MLA_NOPE = 128
MLA_ROPE = 64
MLA_V = 128
MLA_QK_PAD = 256
ROPE_THETA = 10000.0
SSM_GROUPS = 32
SSM_GROUP_IN = 16
SSM_STATE = 64
SSM_CHUNK = 64
REL_BUCKETS = 32
REL_MAX_DIST = 128
XA_HEADS = 4
XA_HEAD_DIM = 128
N_EXPERT_GROUPS = 4
EXPERTS_PER_GROUP = 8
N_EXPERTS = 32
EXPERT_FF = 512
EXPERT_BLOCK = 256
NORM_EPS = 1e-6
NEG_INF = -1e30
LOG2E = 1.4426950408889634

PA_W = 768
PB_W = 1024
PC_W = 896
PD_W = 512
P_SEGS = (PA_W, PB_W, PC_W, PD_W)

V7X_VMEM_LIMIT = 48 * 1024 * 1024

NT_DIMS = (((1,), (1,)), ((), ()))


def _cparams(sem, vmem=V7X_VMEM_LIMIT):
    return pltpu.CompilerParams(dimension_semantics=sem, vmem_limit_bytes=vmem)


def _const_spec(shape):
    nd = len(shape)
    return pl.BlockSpec(shape, lambda *_: (0,) * nd, pipeline_mode=pl.Buffered(1))


def _rms(x, g):
    return x * lax.rsqrt(jnp.mean(x * x, axis=-1, keepdims=True) + NORM_EPS) * g


def _dot(a, b):
    return jnp.dot(a, b, preferred_element_type=F32)


def _dot_nt(a, b):
    return lax.dot_general(a, b, NT_DIMS, preferred_element_type=F32)


def _head_rms_mxu(x, gains, width):
    n = x.shape[-1]
    same = (lax.broadcasted_iota(jnp.int32, (n, n), 0) // width == lax.broadcasted_iota(jnp.int32, (n, n), 1) // width)
    ones = same.astype(BF16)
    x2 = x * x
    hi = x2.astype(BF16)
    lo = (x2 - hi.astype(F32)).astype(BF16)
    ss = _dot(hi, ones) + _dot(lo, ones)
    return x * lax.rsqrt(ss * (1.0 / width) + NORM_EPS) * gains


def _norm_proj_kernel(x_ref, g_ref, w_ref, *refs, segs, n_t):
    h = _rms(x_ref[...], g_ref[...]).astype(BF16)
    o_refs = refs[1:] if n_t else refs
    off = 0
    for o_ref, n in zip(o_refs, segs):
        o_ref[...] = _dot(h, w_ref[:, off:off + n])
        off += n
    if n_t:
        o_refs[len(segs)][...] = _dot_nt(refs[0][...], h).astype(BF16)


def _norm_proj(x, g, w, segs, tm, wt=None):
    s, d = x.shape
    n = sum(segs)
    n_t = 0 if wt is None else wt.shape[0]
    in_specs = [pl.BlockSpec((tm, d), lambda i: (i, 0)), _const_spec((1, d)), _const_spec((d, n))]
    out_shape = [jax.ShapeDtypeStruct((s, n_i), F32) for n_i in segs]
    out_specs = [pl.BlockSpec((tm, n_i), lambda i: (i, 0)) for n_i in segs]
    args = [x, g.reshape(1, d), w]
    if n_t:
        in_specs.append(_const_spec((n_t, d)))
        out_shape.append(jax.ShapeDtypeStruct((n_t, s), BF16))
        out_specs.append(pl.BlockSpec((n_t, tm), lambda i: (0, i)))
        args.append(wt)
    return pl.pallas_call(
        functools.partial(_norm_proj_kernel, segs=segs, n_t=n_t),
        out_shape=out_shape,
        grid=(s // tm,),
        in_specs=in_specs,
        out_specs=out_specs,
        compiler_params=_cparams(("parallel",)),
        name="norm_proj",
    )(*args)


def _t5_buckets():
    qi = np.arange(SWA_BLOCK)[:, None]
    kj = np.arange(2 * SWA_BLOCK)[None, :]
    dist = SWA_BLOCK + qi - kj
    valid = (dist >= 0) & (dist < SWA_BLOCK)
    max_exact = REL_BUCKETS // 2
    d0 = np.maximum(dist, 0)
    dl = np.maximum(dist, 1).astype(np.float32)
    large = max_exact + (np.log(dl / max_exact) / np.log(REL_MAX_DIST / max_exact)
                         * (REL_BUCKETS - max_exact)).astype(np.int32)
    large = np.minimum(large, REL_BUCKETS - 1)
    bucket = np.where(d0 < max_exact, d0, large).astype(np.int32)
    return bucket, valid.astype(np.int32)


def _bias_kernel(rb_ref, bucket_ref, valid_ref, o_ref):
    b = bucket_ref[...]
    ok = valid_ref[...] > 0
    for h in range(SWA_HEADS):
        acc = jnp.zeros(b.shape, F32)
        for k in range(REL_BUCKETS):
            acc = jnp.where(b == k, rb_ref[k, h], acc)
        o_ref[h] = jnp.where(ok, acc, NEG_INF)


def _bias_table(rel_bias):
    bucket, valid = _t5_buckets()
    tab = pl.pallas_call(
        _bias_kernel,
        out_shape=jax.ShapeDtypeStruct((SWA_HEADS, SWA_BLOCK, 2 * SWA_BLOCK), F32),
        in_specs=[pl.BlockSpec(memory_space=pltpu.SMEM),
                  pl.BlockSpec(memory_space=pltpu.VMEM), pl.BlockSpec(memory_space=pltpu.VMEM)],
        out_specs=pl.BlockSpec(memory_space=pltpu.VMEM),
        name="t5_bias_table",
    )(rel_bias, jnp.asarray(bucket), jnp.asarray(valid))
    grp = SWA_HEADS // SWA_KV_HEADS
    tab = tab.reshape(SWA_KV_HEADS, grp, SWA_BLOCK, 2 * SWA_BLOCK).transpose(0, 3, 1, 2)
    return tab.reshape(SWA_KV_HEADS, 2 * SWA_BLOCK, grp * SWA_BLOCK)


SWA_QB = 4


def _swa_kernel(sink_ref, cur_ref, prev_ref, vt_cur_ref, vt_prev_ref, qg_ref, kg_ref, bias_ref, o_ref, ot_scr):
    i = pl.program_id(0)
    blk, dh = SWA_BLOCK, SWA_HEAD_DIM
    grp = SWA_HEADS // SWA_KV_HEADS
    scale = dh ** -0.5

    qn = _head_rms_mxu(cur_ref[:, 0:512], qg_ref[...], dh) * scale
    k_all = jnp.concatenate([prev_ref[:, 512:640], cur_ref[:, 512:640]], axis=0)
    kn = _head_rms_mxu(k_all, kg_ref[...], dh)
    vt_all = jnp.concatenate([vt_prev_ref[...], vt_cur_ref[...]], axis=1)
    lane_k = lax.broadcasted_iota(jnp.int32, kn.shape, 1)
    lane_q = lax.broadcasted_iota(jnp.int32, (blk, 128), 1)
    key_row = lax.broadcasted_iota(jnp.int32, (2 * blk, grp * blk), 0)
    for kh in range(SWA_KV_HEADS):
        k_one = jnp.where(lane_k // dh == kh, kn, 0.0)
        k_dup = (k_one + pltpu.roll(k_one, dh, 1)).astype(BF16)
        sink = jnp.concatenate([jnp.full((1, blk), sink_ref[kh * grp + g], F32) for g in range(grp)], axis=1)
        for j in range(SWA_QB):
            rows = slice(j * blk, (j + 1) * blk)
            qs = []
            for g in range(grp):
                h = kh * grp + g
                q_pair = qn[rows, (h // 2) * 128:(h // 2 + 1) * 128]
                qs.append(jnp.where(lane_q // dh == h % 2, q_pair, 0.0))
            q_st = jnp.concatenate(qs, axis=0).astype(BF16)
            st = _dot_nt(k_dup[j * blk:(j + 2) * blk], q_st) + bias_ref[kh]
            if j == 0:
                st = jnp.where(jnp.logical_and(i == 0, key_row < blk), NEG_INF, st)
            m = jnp.maximum(jnp.max(st, axis=0, keepdims=True), sink)
            p = jnp.exp(st - m)
            denom = jnp.sum(p, axis=0, keepdims=True) + jnp.exp(sink - m)
            ot = _dot(vt_all[kh * dh:(kh + 1) * dh, j * blk:(j + 2) * blk], p.astype(BF16)) / denom
            for g in range(grp):
                h = kh * grp + g
                ot_scr[h * dh:(h + 1) * dh, rows] = ot[:, g * blk:(g + 1) * blk]
    o_ref[...] = ot_scr[...].T


def _swa(pa, vt, qn_g, kn_g, sinks, bias_t):
    s = pa.shape[0]
    tq = SWA_QB * SWA_BLOCK
    grp = SWA_HEADS // SWA_KV_HEADS
    prev = lambda i, sk: (jnp.maximum(i * SWA_QB - 1, 0), 0)
    return pl.pallas_call(
        _swa_kernel,
        out_shape=jax.ShapeDtypeStruct((s, GROUP_WIDTH), F32),
        grid_spec=pltpu.PrefetchScalarGridSpec(
            num_scalar_prefetch=1,
            grid=(s // tq,),
            in_specs=[pl.BlockSpec((tq, PA_W), lambda i, sk: (i, 0)),
                      pl.BlockSpec((SWA_BLOCK, PA_W), prev),
                      pl.BlockSpec((128, tq), lambda i, sk: (0, i)),
                      pl.BlockSpec((128, SWA_BLOCK), lambda i, sk: (0, jnp.maximum(i * SWA_QB - 1, 0))),
                      _const_spec((1, GROUP_WIDTH)), _const_spec((1, 128)),
                      _const_spec((SWA_KV_HEADS, 2 * SWA_BLOCK, grp * SWA_BLOCK))],
            out_specs=pl.BlockSpec((tq, GROUP_WIDTH), lambda i, sk: (i, 0)),
            scratch_shapes=[pltpu.VMEM((GROUP_WIDTH, tq), F32)],
        ),
        compiler_params=_cparams(("parallel",)),
        name="swa_attention",
    )(sinks, pa, pa, vt, vt, jnp.tile(qn_g, SWA_HEADS).reshape(1, -1), jnp.tile(kn_g, SWA_KV_HEADS).reshape(1, -1),
      bias_t)


CONV_TILE = 256
CONV_HALO = 32


def _conv_kernel(cur_ref, prev_ref, dww_ref, dwb_ref, lng_ref, lnb_ref, pww_ref, pwb_ref, o_ref, z_scr, zph_scr):
    i = pl.program_id(0)
    t, c = CONV_TILE, GROUP_WIDTH
    z_scr[CONV_HALO:CONV_HALO + t, :] = cur_ref[:, 0:c] * jax.nn.sigmoid(cur_ref[:, c:2 * c])
    zp = prev_ref[:, 0:c] * jax.nn.sigmoid(prev_ref[:, c:2 * c])
    z_scr[0:CONV_HALO, :] = jnp.where(i > 0, zp, 0.0)
    base = CONV_HALO - (CONV_WIDTH - 1)
    span = t + CONV_HALO - 8
    for ph in range(1, 8):
        zph_scr[ph - 1] = z_scr[ph:ph + span, :]
    acc = jnp.zeros((t, c), F32)
    for k in range(CONV_WIDTH):
        ph, off = (base + k) % 8, (base + k) // 8 * 8
        tap = z_scr[off:off + t, :] if ph == 0 else zph_scr[ph - 1, off:off + t, :]
        acc = acc + dww_ref[k:k + 1, :] * tap
    acc = acc + dwb_ref[...]
    xc = acc - jnp.mean(acc, axis=-1, keepdims=True)
    y = xc * lax.rsqrt(jnp.mean(xc * xc, axis=-1, keepdims=True) + NORM_EPS) * lng_ref[...] + lnb_ref[...]
    y = y * jax.nn.sigmoid(y)
    o_ref[...] = _dot(y.astype(BF16), pww_ref[...]) + pwb_ref[...]


def _conv(pb, dw_w, dw_b, ln_g, ln_b, pw_w, pw_b):
    s = pb.shape[0]
    c = GROUP_WIDTH
    r = CONV_TILE // CONV_HALO
    row = lambda v: v.reshape(1, c)
    return pl.pallas_call(
        _conv_kernel,
        out_shape=jax.ShapeDtypeStruct((s, c), F32),
        grid=(s // CONV_TILE,),
        in_specs=[pl.BlockSpec((CONV_TILE, PB_W), lambda i: (i, 0)),
                  pl.BlockSpec((CONV_HALO, PB_W), lambda i: (jnp.maximum(i * r - 1, 0), 0)),
                  _const_spec((CONV_WIDTH, c)), _const_spec((1, c)), _const_spec((1, c)), _const_spec((1, c)),
                  _const_spec((c, c)), _const_spec((1, c))],
        out_specs=pl.BlockSpec((CONV_TILE, c), lambda i: (i, 0)),
        scratch_shapes=[pltpu.VMEM((CONV_TILE + CONV_HALO, c), F32),
                        pltpu.VMEM((7, CONV_TILE + CONV_HALO - 8, c), F32)],
        compiler_params=_cparams(("parallel",)),
        name="conformer_conv",
    )(pb, pb, dw_w, row(dw_b), row(ln_g), row(ln_b), pw_w, row(pw_b))


MLA_TILE = 256


def _rot_half64(x):
    n = x.shape[-1]
    ax = x.ndim - 1
    lane = lax.broadcasted_iota(jnp.int32, x.shape, ax)
    return jnp.where((lane % 64) < 32, -pltpu.roll(x, n - 32, ax), pltpu.roll(x, 32, ax))


def _seg_rms64(x):
    n = x.shape[-1]
    lane = lax.broadcasted_iota(jnp.int32, x.shape, x.ndim - 1)
    x2 = x * x
    r = jnp.zeros(x.shape, F32)
    for h in range(n // 64):
        msk = jnp.logical_and(lane >= h * 64, lane < (h + 1) * 64)
        sm = jnp.sum(jnp.where(msk, x2, 0.0), axis=-1, keepdims=True)
        r = jnp.where(msk, lax.rsqrt(sm * (1.0 / 64) + NORM_EPS), r)
    return x * r


def _mla_proj_kernel(pc_ref, cos_ref, sin_ref, qag_ref, kvag_ref, wq_ref, wkv_ref, wvt_ref,
                     qng_ref, qrg_ref, kng_ref, krg_ref, q_ref, k_ref, v_ref):
    h_n, dn = MLA_HEADS, MLA_NOPE
    scale = (MLA_NOPE + MLA_ROPE) ** -0.5 * LOG2E
    cos4 = cos_ref[...]
    sin4 = sin_ref[...]
    qn = _rms(pc_ref[:, 0:512], qag_ref[...]).astype(BF16)
    q = _dot(qn, wq_ref[...])
    kvn = _rms(pc_ref[:, 512:768], kvag_ref[...]).astype(BF16)
    kv = _dot(kvn, wkv_ref[:, 0:h_n * dn])
    vt_all = _dot_nt(wvt_ref[...], kvn).astype(BF16)
    qr = _seg_rms64(q[:, h_n * dn:]) * qrg_ref[...]
    qr = (qr * cos4 + _rot_half64(qr) * sin4) * scale
    kr = pc_ref[:, 768:896]
    kr = kr * lax.rsqrt(jnp.sum(kr * kr, axis=-1, keepdims=True) * (1.0 / MLA_ROPE) + NORM_EPS) * krg_ref[...]
    kr = (kr * cos4[:, 0:128] + _rot_half64(kr) * sin4[:, 0:128]).astype(BF16)
    lane = lax.broadcasted_iota(jnp.int32, (q.shape[0], 128), 1)
    for h in range(h_n):
        qnope = _rms(q[:, h * dn:(h + 1) * dn], qng_ref[...]) * scale
        blk = qr[:, (h // 2) * 128:(h // 2 + 1) * 128]
        if h % 2 == 1:
            blk = pltpu.roll(blk, 64, 1)
        blk = jnp.where(lane < 64, blk, 0.0)
        q_ref[h] = jnp.concatenate([qnope, blk], axis=-1).astype(BF16)
        knope = _rms(kv[:, h * dn:(h + 1) * dn], kng_ref[...]).astype(BF16)
        k_ref[h] = jnp.concatenate([knope, kr], axis=-1)
        v_ref[h, 0] = vt_all[h * dn:(h + 1) * dn, :]


def _mla_proj(pc, cos4, sin4, qa_g, kva_g, wq, wkv, qn_nope_g, qn_rope_g4, kn_nope_g, kn_rope_g_pad):
    s = pc.shape[0]
    t = MLA_TILE
    h = MLA_HEADS
    per_blk = FLASH_BK // t
    wk, wvt = wkv[:, 0:h * MLA_NOPE], wkv[:, h * MLA_NOPE:].T
    row = lambda v: v.reshape(1, -1)
    return pl.pallas_call(
        _mla_proj_kernel,
        out_shape=[jax.ShapeDtypeStruct((h, s, MLA_QK_PAD), BF16),
                   jax.ShapeDtypeStruct((h, s, MLA_QK_PAD), BF16),
                   jax.ShapeDtypeStruct((h, s // FLASH_BK, MLA_V, FLASH_BK), BF16)],
        grid=(s // t,),
        in_specs=[pl.BlockSpec((t, PC_W), lambda i: (i, 0)),
                  pl.BlockSpec((t, 256), lambda i: (i, 0)), pl.BlockSpec((t, 256), lambda i: (i, 0)),
                  _const_spec((1, 512)), _const_spec((1, 256)),
                  _const_spec((512, 768)), _const_spec((256, h * MLA_NOPE)), _const_spec((h * MLA_V, 256)),
                  _const_spec((1, 128)), _const_spec((1, 256)), _const_spec((1, 128)), _const_spec((1, 128))],
        out_specs=[pl.BlockSpec((h, t, MLA_QK_PAD), lambda i: (0, i, 0)),
                   pl.BlockSpec((h, t, MLA_QK_PAD), lambda i: (0, i, 0)),
                   pl.BlockSpec((h, 1, MLA_V, t), lambda i: (0, i // per_blk, 0, i % per_blk))],
        compiler_params=_cparams(("parallel",)),
        name="mla_proj",
    )(pc, cos4, sin4, row(qa_g), row(kva_g), wq, wk, wvt, row(qn_nope_g), row(qn_rope_g4), row(kn_nope_g),
      row(kn_rope_g_pad))


FLASH_BQ = 2048
FLASH_BK = 512
FLASH_SUM_ROWS = 16


def _flash_kernel(q_ref, k_ref, vt_ref, o_ref, sa_scr, sb_scr, m_scr, acc_scr):
    bq, bk = FLASH_BQ, FLASH_BK
    per_q = bq // bk
    assert per_q % 2 == 0
    qi = pl.program_id(1)
    m_scr[...] = jnp.full(m_scr.shape, NEG_INF, F32)
    acc_scr[...] = jnp.zeros(acc_scr.shape, F32)
    ones_rows = jnp.ones((FLASH_SUM_ROWS, bk), BF16)

    def scores(j, s_scr, c0=0):
        k = k_ref[pl.ds(pl.multiple_of(j * bk, bk), bk), :]
        s_scr[:, c0:] = _dot_nt(k, q_ref[c0:, :])

    def update(j, s_scr, masked, c0=0, c1=bq):
        st = s_scr[:, c0:c1]
        if masked:
            kpos = j * bk + lax.broadcasted_iota(jnp.int32, st.shape, 0)
            qpos = qi * bq + c0 + lax.broadcasted_iota(jnp.int32, st.shape, 1)
            st = jnp.where(kpos <= qpos, st, NEG_INF)
        m_prev = m_scr[:, c0:c1]
        m_new = jnp.maximum(m_prev, jnp.max(st, axis=0, keepdims=True))
        alpha = jnp.exp2(m_prev - m_new)
        p = jnp.exp2(st - m_new)
        vt_ext = jnp.concatenate([vt_ref[j], ones_rows], axis=0)
        acc_scr[:, c0:c1] = alpha * acc_scr[:, c0:c1] + _dot(vt_ext, p.astype(BF16))
        m_scr[:, c0:c1] = m_new

    scores(0, sa_scr)

    def body(i, carry):
        for c in range(0, per_q, 2):
            scores(per_q * i + c + 1, sb_scr)
            update(per_q * i + c, sa_scr, False)
            scores(per_q * i + c + 2, sa_scr)
            update(per_q * i + c + 1, sb_scr, False)
        return carry

    lax.fori_loop(0, qi, body, 0)
    first = qi * per_q
    bufs = (sa_scr, sb_scr)
    for c in range(per_q):
        if c + 1 < per_q:
            scores(first + c + 1, bufs[(c + 1) % 2], (c + 1) * bk)
        update(first + c, bufs[c % 2], True, c * bk, (c + 1) * bk)
        if c + 1 < per_q:
            update(first + c, bufs[c % 2], False, (c + 1) * bk, bq)
    o_ref[...] = (acc_scr[0:MLA_V, :] / acc_scr[MLA_V:MLA_V + 1, :]).T


def _flash(q, k, vt):
    h, s, _ = q.shape
    nkb = s // FLASH_BK
    return pl.pallas_call(
        _flash_kernel,
        out_shape=jax.ShapeDtypeStruct((s, h * MLA_V), F32),
        grid=(h, s // FLASH_BQ),
        in_specs=[pl.BlockSpec((None, FLASH_BQ, MLA_QK_PAD), lambda hh, i: (hh, i, 0)),
                  pl.BlockSpec((None, s, MLA_QK_PAD), lambda hh, i: (hh, 0, 0)),
                  pl.BlockSpec((None, nkb, MLA_V, FLASH_BK), lambda hh, i: (hh, 0, 0, 0))],
        out_specs=pl.BlockSpec((FLASH_BQ, MLA_V), lambda hh, i: (i, hh)),
        scratch_shapes=[pltpu.VMEM((FLASH_BK, FLASH_BQ), F32), pltpu.VMEM((FLASH_BK, FLASH_BQ), F32),
                        pltpu.VMEM((1, FLASH_BQ), F32), pltpu.VMEM((MLA_V + FLASH_SUM_ROWS, FLASH_BQ), F32)],
        compiler_params=_cparams(("parallel", "arbitrary"), vmem=56 * 1024 * 1024),
        name="mla_flash",
    )(q, k, vt)


def _split3_dot(a, sel):
    a1 = a.astype(BF16)
    r1 = a - a1.astype(F32)
    a2 = r1.astype(BF16)
    a3 = (r1 - a2.astype(F32)).astype(BF16)
    return _dot(a1, sel) + _dot(a2, sel) + _dot(a3, sel)


def _hi_dot(a, b):
    return lax.dot_general(a, b, (((1,), (0,)), ((), ())), precision=lax.Precision.HIGHEST,
                           preferred_element_type=F32)


def _ssm_kernel(u_ref, arc_ref, aic_ref, arr_ref, air_ref, ldt_ref, bre_ref, bim_ref, btre_ref, btim_ref,
                ctre_ref, ctim_ref, dcol_ref, y_ref, toep_scr, *, nch):
    tc, n_in, n_st = SSM_CHUNK, SSM_GROUP_IN, SSM_STATE
    w = tc * n_in
    dt = jnp.exp(ldt_ref[...])
    ar_c, ai_c = arc_ref[...], aic_ref[...]
    ar_r, ai_r = arr_ref[...], air_ref[...]

    def lam_pow(ar, ai, e):
        mag = jnp.exp(ar * dt * e)
        ang = ai * dt * e
        return mag * jnp.cos(ang), mag * jnp.sin(ang)

    def zoh_coef(ar, ai):
        lr, li = lam_pow(ar, ai, 1.0)
        x, y = lr - 1.0, li
        den = ar * ar + ai * ai
        return (x * ar + y * ai) / den, (y * ar - x * ai) / den

    lane_w = lax.broadcasted_iota(jnp.int32, (tc, w), 1)
    sel_tau = (lane_w // n_in == lax.broadcasted_iota(jnp.int32, (tc, w), 0)).astype(BF16)
    lane_o = lax.broadcasted_iota(jnp.int32, (n_in, w), 1)
    sel_o = (lane_o % n_in == lax.broadcasted_iota(jnp.int32, (n_in, w), 0)).astype(BF16)

    tau = lax.broadcasted_iota(jnp.int32, (n_st, tc), 1).astype(F32)
    lre, lim = lam_pow(ar_c, ai_c, tau)
    lre_x, lim_x = _split3_dot(lre, sel_tau), _split3_dot(lim, sel_tau)
    rre, rim = lam_pow(ar_c, ai_c, (tc - 1.0) - tau)
    rre_x, rim_x = _split3_dot(rre, sel_tau), _split3_dot(rim, sel_tau)
    ctre_x, ctim_x = _split3_dot(ctre_ref[...], sel_o), _split3_dot(ctim_ref[...], sel_o)

    e_re = lre_x * ctre_x - lim_x * ctim_x
    e_im = lre_x * ctim_x + lim_x * ctre_x

    cr_c, ci_c = zoh_coef(ar_c, ai_c)
    bb_re = cr_c * bre_ref[...] - ci_c * bim_ref[...]
    bb_im = cr_c * bim_ref[...] + ci_c * bre_ref[...]
    bb_re_x, bb_im_x = _split3_dot(bb_re, sel_o), _split3_dot(bb_im, sel_o)
    f_re = (rre_x * bb_re_x - rim_x * bb_im_x).astype(BF16)
    f_im = (rre_x * bb_im_x + rim_x * bb_re_x).astype(BF16)

    cr_r, ci_r = zoh_coef(ar_r, ai_r)
    bbt_re = cr_r * btre_ref[...] - ci_r * btim_ref[...]
    bbt_im = cr_r * btim_ref[...] + ci_r * btre_ref[...]
    g_row = _hi_dot(bbt_re, e_re) - _hi_dot(bbt_im, e_im)
    lane_g = lax.broadcasted_iota(jnp.int32, (n_in, w), 1)
    row_g = lax.broadcasted_iota(jnp.int32, (n_in, w), 0)
    g_row = g_row + jnp.where(lane_g == row_g, dcol_ref[...], 0.0)
    for s in range(tc):
        rolled = g_row if s == 0 else pltpu.roll(g_row, n_in * s, 1)
        toep_scr[s * n_in:(s + 1) * n_in, :] = jnp.where(lane_g >= n_in * s, rolled, 0.0).astype(BF16)

    u = u_ref[...]
    s_re = _dot_nt(u, f_re)
    s_im = _dot_nt(u, f_im)
    row = lax.broadcasted_iota(jnp.int32, (nch, n_st), 0)
    sh = 1
    while sh < nch:
        pr, pi = lam_pow(ar_r, ai_r, float(tc * sh))
        re_s = jnp.where(row >= sh, pltpu.roll(s_re, sh, 0), 0.0)
        im_s = jnp.where(row >= sh, pltpu.roll(s_im, sh, 0), 0.0)
        s_re, s_im = s_re + pr * re_s - pi * im_s, s_im + pr * im_s + pi * re_s
        sh *= 2
    p_re = jnp.where(row >= 1, pltpu.roll(s_re, 1, 0), 0.0)
    p_im = jnp.where(row >= 1, pltpu.roll(s_im, 1, 0), 0.0)
    l1r, l1i = lam_pow(ar_c, ai_c, 1.0)
    h_re = (l1r * e_re - l1i * e_im).astype(BF16)
    h_im = (l1r * e_im + l1i * e_re).astype(BF16)
    y = _dot(u, toep_scr[...]) + _dot(p_re.astype(BF16), h_re) - _dot(p_im.astype(BF16), h_im)
    y_ref[...] = y


def _ssm(u_r, a_re, a_im, log_dt, b_re, b_im, c_re, c_im, d_skip):
    g, nch, w = u_r.shape
    n_st, n_in = SSM_STATE, SSM_GROUP_IN
    col = lambda v: v.reshape(g, n_st, 1)
    rw = lambda v: v.reshape(g, 1, n_st)
    spec = lambda *shape: pl.BlockSpec((None,) + shape, lambda i: (i,) + (0,) * len(shape))
    return pl.pallas_call(
        functools.partial(_ssm_kernel, nch=nch),
        out_shape=jax.ShapeDtypeStruct((g, nch, w), F32),
        grid=(g,),
        in_specs=[spec(nch, w), spec(n_st, 1), spec(n_st, 1), spec(1, n_st), spec(1, n_st), spec(1, 1),
                  spec(n_st, n_in), spec(n_st, n_in), spec(n_in, n_st), spec(n_in, n_st),
                  spec(n_st, n_in), spec(n_st, n_in), spec(n_in, 1)],
        out_specs=spec(nch, w),
        scratch_shapes=[pltpu.VMEM((w, w), BF16)],
        compiler_params=_cparams(("parallel",)),
        name="s5_ssm",
    )(u_r, col(a_re), col(a_im), rw(a_re), rw(a_im), log_dt.reshape(g, 1, 1), b_re, b_im,
      jnp.swapaxes(b_re, 1, 2), jnp.swapaxes(b_im, 1, 2), jnp.swapaxes(c_re, 1, 2), jnp.swapaxes(c_im, 1, 2),
      d_skip.reshape(g, n_in, 1))


OUT_TILE = 256


def _mix_out_kernel(x_ref, oa_ref, ob_ref, oc_ref, yd_ref, gw_ref, gb_ref, mg_ref, wo_ref, o_ref):
    c = GROUP_WIDTH
    y = jax.nn.gelu(yd_ref[...])
    od = y * jax.nn.sigmoid(_dot(y.astype(BF16), gw_ref[...]) + gb_ref[...])
    acc = x_ref[...]
    for m, o in enumerate((oa_ref[...], ob_ref[...], oc_ref[...], od)):
        n = _rms(o, mg_ref[:, m * c:(m + 1) * c]).astype(BF16)
        acc = acc + _dot(n, wo_ref[m * c:(m + 1) * c, :])
    o_ref[...] = acc


def _mix_out(x, oa, ob, oc, yd, glu_w, glu_b, mix_g, w_out):
    s, d = x.shape
    c = GROUP_WIDTH
    t = OUT_TILE
    tile = lambda n: pl.BlockSpec((t, n), lambda i: (i, 0))
    return pl.pallas_call(
        _mix_out_kernel,
        out_shape=jax.ShapeDtypeStruct((s, d), F32),
        grid=(s // t,),
        in_specs=[tile(d), tile(c), tile(c), tile(c), tile(c),
                  _const_spec((c, c)), _const_spec((1, c)), _const_spec((1, d)), _const_spec((d, d))],
        out_specs=tile(d),
        compiler_params=_cparams(("parallel",)),
        name="mix_out",
    )(x, oa, ob, oc, yd, glu_w, glu_b.reshape(1, c), mix_g.reshape(1, d), w_out)


XA_TILE = 256


def _xattn_kernel(x_ref, lg_ref, wq_ref, qg_ref, k_ref, vt_ref, kg_ref, wo_ref, o_ref):
    dh = XA_HEAD_DIM
    x = x_ref[...]
    q = _dot(_rms(x, lg_ref[...]).astype(BF16), wq_ref[...])
    qn = (_head_rms_mxu(q, qg_ref[...], dh) * dh ** -0.5).astype(BF16)
    kn = _head_rms_mxu(k_ref[...], kg_ref[...], dh).astype(BF16)
    outs = []
    for h in range(XA_HEADS):
        cols = slice(h * dh, (h + 1) * dh)
        st = _dot_nt(kn[:, cols], qn[:, cols])
        p = jnp.exp(st - jnp.max(st, axis=0, keepdims=True))
        outs.append(_dot(vt_ref[cols, :], p.astype(BF16)) / jnp.sum(p, axis=0, keepdims=True))
    o = jnp.concatenate(outs, axis=0).T
    o_ref[...] = x + _dot(o.astype(BF16), wo_ref[...])


def _xattn(x, ln_g, wq, qn_g, k_mem, vt_mem, kn_g, wo):
    s, d = x.shape
    m, hw = k_mem.shape
    t = XA_TILE
    return pl.pallas_call(
        _xattn_kernel,
        out_shape=jax.ShapeDtypeStruct((s, d), F32),
        grid=(s // t,),
        in_specs=[pl.BlockSpec((t, d), lambda i: (i, 0)), _const_spec((1, d)), _const_spec((d, hw)),
                  _const_spec((1, hw)), _const_spec((m, hw)), _const_spec((hw, m)), _const_spec((1, hw)),
                  _const_spec((hw, d))],
        out_specs=pl.BlockSpec((t, d), lambda i: (i, 0)),
        compiler_params=_cparams(("parallel",)),
        name="mem_xattn",
    )(x, ln_g.reshape(1, d), wq, jnp.tile(qn_g, XA_HEADS).reshape(1, -1), k_mem, vt_mem,
      jnp.tile(kn_g, XA_HEADS).reshape(1, -1), wo)


ROUTER_TILE = 256
ROUTER_LANES = 128


def _router_kernel(x_ref, lg_ref, whi_ref, wlo_ref, b_ref, h_ref, r_ref, cnt_ref, carry_scr):
    i = pl.program_id(0)
    t = ROUTER_TILE
    ng, epg = N_EXPERT_GROUPS, EXPERTS_PER_GROUP

    @pl.when(i == 0)
    def _():
        carry_scr[...] = jnp.zeros(carry_scr.shape, F32)

    h = _rms(x_ref[...], lg_ref[...])
    h_hi = h.astype(BF16)
    h_lo = (h - h_hi.astype(F32)).astype(BF16)
    _rows_store(h_ref, 0, t, _pack_bf16_pair(h[:, 0:D_MODEL // 2], h[:, D_MODEL // 2:]))
    logits = _dot(h_hi, whi_ref[...]) + _dot(h_hi, wlo_ref[...]) + _dot(h_lo, whi_ref[...]) + b_ref[...]
    lane = lax.broadcasted_iota(jnp.int32, (t, ROUTER_LANES), 1)
    big = jnp.int32(ROUTER_LANES)

    def first_max(vals, mask):
        v = jnp.where(mask, vals, -jnp.inf)
        mx = jnp.max(v, axis=-1, keepdims=True)
        idx = jnp.min(jnp.where(jnp.logical_and(mask, v == mx), lane, big), axis=-1, keepdims=True)
        return mx, idx

    gmask = lane < ng
    gmax, gidx = first_max(logits, gmask)
    g_w = 1.0 / jnp.sum(jnp.where(gmask, jnp.exp(logits - gmax), 0.0), axis=-1, keepdims=True)
    emask = jnp.logical_and(lane >= ng + gidx * epg, lane < ng + (gidx + 1) * epg)
    v1, i1 = first_max(logits, emask)
    v2, i2 = first_max(logits, jnp.logical_and(emask, lane != i1))
    e2 = jnp.exp(v2 - v1)
    gate1 = g_w / (1.0 + e2)
    gate2 = g_w * e2 / (1.0 + e2)

    oh1 = lane == i1
    oh2 = lane == i2
    both = (oh1.astype(F32) + oh2.astype(F32))
    tri = (lax.broadcasted_iota(jnp.int32, (t, t), 1) < lax.broadcasted_iota(jnp.int32, (t, t), 0)).astype(BF16)
    before = _dot(tri, both.astype(BF16)) + carry_scr[...]
    rank1 = jnp.sum(jnp.where(oh1, before, 0.0), axis=-1, keepdims=True)
    rank2 = jnp.sum(jnp.where(oh2, before, 0.0), axis=-1, keepdims=True)
    carry_scr[...] = carry_scr[...] + jnp.sum(both, axis=0, keepdims=True)
    cnt_ref[...] = jnp.broadcast_to(carry_scr[...], cnt_ref.shape)

    out = jnp.zeros((t, ROUTER_LANES), F32)
    for k, val in enumerate(((i1 - ng).astype(F32), (i2 - ng).astype(F32), gate1, gate2, rank1, rank2)):
        out = jnp.where(lane == k, val, out)
    r_ref[...] = out


def _router(x, ln_g, w_hi, w_lo, bias):
    s, d = x.shape
    t = ROUTER_TILE
    return pl.pallas_call(
        _router_kernel,
        out_shape=[jax.ShapeDtypeStruct((s * ROW_SUB, 128), jnp.uint32), jax.ShapeDtypeStruct((s, ROUTER_LANES), F32),
                   jax.ShapeDtypeStruct((8, ROUTER_LANES), F32)],
        grid=(s // t,),
        in_specs=[pl.BlockSpec((t, d), lambda i: (i, 0)), _const_spec((1, d)),
                  _const_spec((d, ROUTER_LANES)), _const_spec((d, ROUTER_LANES)), _const_spec((1, ROUTER_LANES))],
        out_specs=[pl.BlockSpec((t * ROW_SUB, 128), lambda i: (i, 0)), pl.BlockSpec((t, ROUTER_LANES), lambda i: (i, 0)),
                   pl.BlockSpec((8, ROUTER_LANES), lambda i: (0, 0))],
        scratch_shapes=[pltpu.VMEM((1, ROUTER_LANES), F32)],
        compiler_params=_cparams(("arbitrary",)),
        name="moe_router",
    )(x, ln_g.reshape(1, d), w_hi, w_lo, bias)


def _ffn_kernel(be_ref, nu_ref, nx_ref, x_ref, wg_hbm, wu_hbm, wd_hbm, o_ref,
                wg_st, wu_st, wd_st, wg_scr, wu_scr, wd_scr, sem, *, layer):
    b = pl.program_id(0)
    half = D_MODEL // 2
    e = be_ref[b]
    prev = be_ref[jnp.maximum(b - 1, 0)]

    def fetch(expert):
        return (pltpu.make_async_copy(wg_hbm.at[layer, expert], wg_st, sem.at[0]),
                pltpu.make_async_copy(wu_hbm.at[layer, expert], wu_st, sem.at[1]),
                pltpu.make_async_copy(wd_hbm.at[layer, expert], wd_st, sem.at[2]))

    @pl.when(b == 0)
    def _():
        for c in fetch(e):
            c.start()

    @pl.when(jnp.logical_or(b == 0, e != prev))
    def _():
        for c in fetch(e):
            c.wait()
        wg_scr[...] = wg_st[...].astype(BF16)
        wu_scr[...] = wu_st[...].astype(BF16)
        wd_scr[...] = wd_st[...].astype(BF16)
        nxt = nx_ref[e]

        @pl.when(nxt != e)
        def _():
            for c in fetch(nxt):
                c.start()

    @pl.when(b < nu_ref[0])
    def _():
        xa, xb = _unpack_bf16_pair(_rows_load(x_ref, 0, EXPERT_BLOCK))
        xa, xb = xa.astype(BF16), xb.astype(BF16)
        g = _dot(xa, wg_scr[0:half, :]) + _dot(xb, wg_scr[half:, :])
        u = _dot(xa, wu_scr[0:half, :]) + _dot(xb, wu_scr[half:, :])
        hid = ((g * jax.nn.sigmoid(g)) * u).astype(BF16)
        _rows_store(o_ref, 0, EXPERT_BLOCK, _pack_bf16_pair(_dot(hid, wd_scr[:, 0:half]), _dot(hid, wd_scr[:, half:])))

    @pl.when(b >= nu_ref[0])
    def _():
        o_ref[...] = jnp.zeros(o_ref.shape, o_ref.dtype)


def _ffn(xs, block_expert, n_used, next_expert, wg, wu, wd, layer):
    d = D_MODEL
    nb = xs.shape[0] // (EXPERT_BLOCK * ROW_SUB)
    blk = (EXPERT_BLOCK * ROW_SUB, 128)
    ff = EXPERT_FF
    return pl.pallas_call(
        functools.partial(_ffn_kernel, layer=layer),
        out_shape=jax.ShapeDtypeStruct(xs.shape, jnp.uint32),
        grid_spec=pltpu.PrefetchScalarGridSpec(
            num_scalar_prefetch=3,
            grid=(nb,),
            in_specs=[pl.BlockSpec(blk, lambda b, be, nu, nx: (b, 0)),
                      pl.BlockSpec(memory_space=pl.ANY), pl.BlockSpec(memory_space=pl.ANY),
                      pl.BlockSpec(memory_space=pl.ANY)],
            out_specs=pl.BlockSpec(blk, lambda b, be, nu, nx: (b, 0)),
            scratch_shapes=[pltpu.VMEM((d, ff), F32), pltpu.VMEM((d, ff), F32), pltpu.VMEM((ff, d), F32),
                            pltpu.VMEM((d, ff), BF16), pltpu.VMEM((d, ff), BF16), pltpu.VMEM((ff, d), BF16),
                            pltpu.SemaphoreType.DMA((3,))],
        ),
        compiler_params=_cparams(("arbitrary",)),
        name="moe_ffn",
    )(block_expert, n_used, next_expert, xs, wg, wu, wd)


MOE_ROW_TILE = 256


ROW_SUB = 8


def _rows_store(ref, base, n, packed):
    for sub in range(ROW_SUB):
        ref[pl.ds(base * ROW_SUB + sub, n, stride=ROW_SUB), :] = packed[:, sub * 128:(sub + 1) * 128]


def _rows_load(ref, base, n):
    return jnp.concatenate([ref[pl.ds(base * ROW_SUB + sub, n, stride=ROW_SUB), :] for sub in range(ROW_SUB)], axis=1)


def _row_tile(idx):
    return pl.ds(pl.multiple_of(idx * ROW_SUB, ROW_SUB), ROW_SUB)


def _pack_bf16_pair(a, b):
    hi = lax.bitcast_convert_type(a.astype(BF16).astype(F32), jnp.uint32)
    lo = lax.bitcast_convert_type(b.astype(BF16).astype(F32), jnp.uint32)
    return hi | (lo >> 16)


def _unpack_bf16_pair(w):
    a = lax.bitcast_convert_type(w & jnp.uint32(0xFFFF0000), F32)
    b = lax.bitcast_convert_type(w << 16, F32)
    return a, b


def _dispatch_kernel(dest_ref, h_ref, init_hbm, xs_hbm, sem):
    del init_hbm
    t = MOE_ROW_TILE

    def row_copy(r, k):
        return pltpu.make_async_copy(h_ref.at[_row_tile(r)], xs_hbm.at[_row_tile(dest_ref[0, 0, 2 * r + k])], sem)

    def issue(r, carry):
        row_copy(r, 0).start(priority=0)
        row_copy(r, 1).start(priority=1)
        return carry

    def drain(r, carry):
        row_copy(r, 0).wait()
        row_copy(r, 1).wait()
        return carry

    lax.fori_loop(0, t, issue, 0, unroll=8)
    lax.fori_loop(0, t, drain, 0, unroll=8)


def _dispatch(h_packed, dest, n_rows):
    s = h_packed.shape[0] // ROW_SUB
    t = MOE_ROW_TILE
    return pl.pallas_call(
        _dispatch_kernel,
        out_shape=jax.ShapeDtypeStruct((n_rows * ROW_SUB, 128), jnp.uint32),
        grid=(s // t,),
        in_specs=[pl.BlockSpec((1, 1, 2 * t), lambda i: (i, 0, 0), memory_space=pltpu.SMEM),
                  pl.BlockSpec((t * ROW_SUB, 128), lambda i: (i, 0)), pl.BlockSpec(memory_space=pl.ANY)],
        out_specs=pl.BlockSpec(memory_space=pl.ANY),
        scratch_shapes=[pltpu.SemaphoreType.DMA(())],
        input_output_aliases={2: 0},
        compiler_params=_cparams(("arbitrary",)),
        name="moe_dispatch",
    )(dest.reshape(s // t, 1, 2 * t), h_packed, jnp.zeros((n_rows * ROW_SUB, 128), jnp.uint32))


def _combine_kernel(dest_ref, x_ref, route_ref, rows_hbm, o_ref, buf, sem):
    t = MOE_ROW_TILE
    half = D_MODEL // 2

    def row_copy(r, k):
        return pltpu.make_async_copy(rows_hbm.at[_row_tile(dest_ref[0, 0, 2 * r + k])], buf.at[_row_tile(k * t + r)], sem)

    def issue(r, carry):
        row_copy(r, 0).start(priority=0)
        row_copy(r, 1).start(priority=1)
        return carry

    def drain(r, carry):
        row_copy(r, 0).wait()
        row_copy(r, 1).wait()
        return carry

    lax.fori_loop(0, t, issue, 0, unroll=8)
    lax.fori_loop(0, t, drain, 0, unroll=8)
    g0 = route_ref[:, 2:3]
    g1 = route_ref[:, 3:4]
    a0, b0 = _unpack_bf16_pair(_rows_load(buf, 0, t))
    a1, b1 = _unpack_bf16_pair(_rows_load(buf, t, t))
    o_ref[:, 0:half] = x_ref[:, 0:half] + g0 * a0 + g1 * a1
    o_ref[:, half:] = x_ref[:, half:] + g0 * b0 + g1 * b1


def _combine(x, route, rows, dest):
    s, d = x.shape
    t = MOE_ROW_TILE
    return pl.pallas_call(
        _combine_kernel,
        out_shape=jax.ShapeDtypeStruct((s, d), F32),
        grid=(s // t,),
        in_specs=[pl.BlockSpec((1, 1, 2 * t), lambda i: (i, 0, 0), memory_space=pltpu.SMEM),
                  pl.BlockSpec((t, d), lambda i: (i, 0)), pl.BlockSpec((t, ROUTER_LANES), lambda i: (i, 0)),
                  pl.BlockSpec(memory_space=pl.ANY)],
        out_specs=pl.BlockSpec((t, d), lambda i: (i, 0)),
        scratch_shapes=[pltpu.VMEM((2 * t * ROW_SUB, 128), jnp.uint32), pltpu.SemaphoreType.DMA(())],
        compiler_params=_cparams(("arbitrary",)),
        name="moe_combine",
    )(dest.reshape(s // t, 1, 2 * t), x, route, rows)


def _rope_tables4(s):
    inv = 1.0 / (ROPE_THETA ** (jnp.arange(0, MLA_ROPE, 2, dtype=F32) / MLA_ROPE))
    ang = jnp.arange(s, dtype=F32)[:, None] * inv[None, :]
    cos, sin = jnp.cos(ang), jnp.sin(ang)
    return jnp.tile(cos, (1, 8)), jnp.tile(sin, (1, 8))


def _moe_layer(x, ln_g, w_group, b_group, w_expert, b_expert, wg, wu, wd, layer):
    s, d = x.shape
    ng, ne = N_EXPERT_GROUPS, N_EXPERTS
    w_r = jnp.zeros((d, ROUTER_LANES), F32).at[:, :ng].set(w_group).at[:, ng:ng + ne].set(w_expert)
    w_hi = w_r.astype(BF16)
    w_lo = (w_r - w_hi.astype(F32)).astype(BF16)
    b_r = jnp.zeros((1, ROUTER_LANES), F32).at[0, :ng].set(b_group).at[0, ng:ng + ne].set(b_expert)
    h_packed, route, cnt = _router(x, ln_g, w_hi, w_lo, b_r)
    e_id = route[:, 0:2].astype(jnp.int32)
    rank = route[:, 4:6].astype(jnp.int32)
    counts = cnt[0, ng:ng + ne].astype(jnp.int32)
    n_blocks = -(-(2 * s + ne * (EXPERT_BLOCK - 1)) // EXPERT_BLOCK)
    padded = (counts + EXPERT_BLOCK - 1) // EXPERT_BLOCK * EXPERT_BLOCK
    pad_ends = jnp.cumsum(padded)
    pad_starts = pad_ends - padded
    dest = pad_starts[e_id] + rank
    block_starts = jnp.arange(n_blocks, dtype=jnp.int32) * EXPERT_BLOCK
    block_expert = jnp.minimum(jnp.sum(pad_ends[None, :] <= block_starts[:, None], axis=1), ne - 1).astype(jnp.int32)
    n_used = (pad_ends[-1] // EXPERT_BLOCK).astype(jnp.int32).reshape(1)
    xs = _dispatch(h_packed, dest, n_blocks * EXPERT_BLOCK)
    experts = jnp.arange(ne, dtype=jnp.int32)
    present = jnp.any(block_expert[None, :] == experts[:, None], axis=1)
    later = jnp.where(jnp.logical_and(present[None, :], experts[None, :] > experts[:, None]), experts[None, :], ne)
    next_expert = jnp.where(jnp.min(later, axis=1) < ne, jnp.min(later, axis=1), experts).astype(jnp.int32)
    rows = _ffn(xs, block_expert, n_used, next_expert, wg, wu, wd, layer)
    return _combine(x, route, rows, dest)


def _perm_w_in(w_in):
    sp = np.cumsum((512, 128, 128, 512, 512, 512, 256, 64, 512))
    a_q, a_k, a_v, b_a, b_g, c_q, c_kv, c_kr, d_u = jnp.split(w_in, [int(v) for v in sp[:-1]], axis=-1)
    zero = jnp.zeros((w_in.shape[0], 64), w_in.dtype)
    return jnp.concatenate([a_q, a_k, a_v, b_a, b_g, c_q, c_kv, c_kr, zero, d_u], axis=-1)


def _perm_heads(w, parts):
    k = w.shape[0]
    wh = w.reshape(k, MLA_HEADS, sum(parts))
    offs = np.cumsum((0,) + tuple(parts))
    return jnp.concatenate([wh[:, :, offs[j]:offs[j + 1]].reshape(k, -1) for j in range(len(parts))], axis=-1)


def kernel(x, mem, rel_bias, ln_mix_g, w_in, swa_qn_g, swa_kn_g, swa_sinks, conv_dw_w, conv_dw_b, conv_ln_g, conv_ln_b, conv_pw_w, conv_pw_b, mla_qa_g, mla_kva_g, mla_wq_b, mla_wkv_b, mla_qn_nope_g, mla_qn_rope_g, mla_kn_nope_g, mla_kn_rope_g, ssm_a_re, ssm_a_im, ssm_log_dt, ssm_b_re, ssm_b_im, ssm_c_re, ssm_c_im, ssm_d, ssm_glu_w, ssm_glu_b, mix_out_g, w_out, ln_cross_g, mem_ln_g, xa_wq, xa_wk, xa_wv, xa_qn_g, xa_kn_g, xa_wo, ln_moe_g, moe_w_group, moe_b_group, moe_w_expert, moe_b_expert, moe_w_gate, moe_w_up, moe_w_down):
    b, s, d = x.shape
    assert b == 1 and d == D_MODEL and s % FLASH_BQ == 0
    xf = x.reshape(s, d)
    memf = mem.reshape(mem.shape[1], d)
    cos4, sin4 = _rope_tables4(s)
    bias_tab = _bias_table(rel_bias)
    nch = s // SSM_CHUNK
    g_n, n_in = SSM_GROUPS, SSM_GROUP_IN
    for l in range(DEPTH):
        pa, pb, pc, pd, vt_a = _norm_proj(xf, ln_mix_g[l], _perm_w_in(w_in[l]).astype(BF16), P_SEGS, 256,
                                          wt=w_in[l][:, 640:768].T.astype(BF16))
        o_a = _swa(pa, vt_a, swa_qn_g[l], swa_kn_g[l], swa_sinks[l], bias_tab)
        o_b = _conv(pb, conv_dw_w[l], conv_dw_b[l], conv_ln_g[l], conv_ln_b[l], conv_pw_w[l].astype(BF16),
                    conv_pw_b[l])
        q, k, vt = _mla_proj(pc, cos4, sin4, mla_qa_g[l], mla_kva_g[l],
                            _perm_heads(mla_wq_b[l], (MLA_NOPE, MLA_ROPE)).astype(BF16),
                            _perm_heads(mla_wkv_b[l], (MLA_NOPE, MLA_V)).astype(BF16),
                            mla_qn_nope_g[l], jnp.tile(mla_qn_rope_g[l], MLA_HEADS), mla_kn_nope_g[l],
                            jnp.concatenate([mla_kn_rope_g[l], jnp.zeros((64,), F32)]))
        o_c = _flash(q, k, vt)
        u_r = pd.astype(BF16).reshape(nch, SSM_CHUNK, g_n, n_in).transpose(2, 0, 1, 3).reshape(g_n, nch, -1)
        y_r = _ssm(u_r, ssm_a_re[l], ssm_a_im[l], ssm_log_dt[l], ssm_b_re[l], ssm_b_im[l], ssm_c_re[l],
                   ssm_c_im[l], ssm_d[l])
        y_d = y_r.reshape(g_n, nch, SSM_CHUNK, n_in).transpose(1, 2, 0, 3).reshape(s, GROUP_WIDTH)
        xf = _mix_out(xf, o_a, o_b, o_c, y_d, ssm_glu_w[l].astype(BF16), ssm_glu_b[l], mix_out_g[l],
                      w_out[l].astype(BF16))
        k_mem, vt_mem = _norm_proj(memf, mem_ln_g[l], xa_wk[l].astype(BF16), (XA_HEADS * XA_HEAD_DIM,), memf.shape[0],
                                   wt=xa_wv[l].T.astype(BF16))
        xf = _xattn(xf, ln_cross_g[l], xa_wq[l].astype(BF16), xa_qn_g[l], k_mem, vt_mem, xa_kn_g[l],
                    xa_wo[l].astype(BF16))
        xf = _moe_layer(xf, ln_moe_g[l], moe_w_group[l], moe_b_group[l], moe_w_expert[l], moe_b_expert[l],
                        moe_w_gate, moe_w_up, moe_w_down, l)
    return xf.reshape(b, s, d)
```
